```python
import math
import jax, jax.numpy as jnp
from jax import lax
import numpy as np

D_MODEL = 1024
BATCH = 1
SEQ = 16384
DEPTH = 4

D_FF = 2816
CONV_K = 4
SSM_EXPAND = 2
SSM_INNER = SSM_EXPAND * D_MODEL
SSM_HEAD_DIM = 64
SSM_HEADS = SSM_INNER // SSM_HEAD_DIM
SSM_GROUPS = 4
SSM_STATE = 128
SSM_CONV_DIM = SSM_INNER + 2 * SSM_GROUPS * SSM_STATE
SSM_CHUNK = 128
DN_HEADS = 8
DN_HEAD_K = 128
DN_HEAD_V = 128
DN_K_DIM = DN_HEADS * DN_HEAD_K
DN_V_DIM = DN_HEADS * DN_HEAD_V
DN_CONV_DIM = 2 * DN_K_DIM + DN_V_DIM
DN_CHUNK = 64
IN_SPLIT_SIZES = (SSM_INNER, SSM_CONV_DIM, SSM_HEADS,
                  DN_CONV_DIM, DN_V_DIM, DN_HEADS, DN_HEADS,
                  D_MODEL, D_MODEL)
IN_DIM = sum(IN_SPLIT_SIZES)
EPS = 1e-6

kernel_name = "hybrid_ssd_gdn_macaron_trunk"


def rms_norm(x, w):
    xf = x.astype(jnp.float32)
    y = xf * lax.rsqrt(jnp.mean(xf * xf, axis=-1, keepdims=True) + EPS)
    return (y * w.astype(jnp.float32)).astype(x.dtype)


def l2_norm(x):
    xf = x.astype(jnp.float32)
    return xf * lax.rsqrt(jnp.sum(xf * xf, axis=-1, keepdims=True) + EPS)


def swiglu(x, w_in, w_out):
    gate, up = jnp.split(x @ w_in, 2, axis=-1)
    return (jax.nn.silu(gate) * up) @ w_out


def causal_depthwise_conv(x, w, bias=None):
    K, C = w.shape
    y = lax.conv_general_dilated(
        x, w[:, None, :].astype(x.dtype), window_strides=(1,), padding=[(K - 1, 0)],
        dimension_numbers=("NWC", "WIO", "NWC"), feature_group_count=C)
    if bias is not None:
        y = y + bias.astype(x.dtype)
    return y


def ssd_chunked(x, dt, a, Bm, Cm):
    b, S, H, P = x.shape
    G, N = Bm.shape[2], Bm.shape[3]
    hg = H // G
    L = SSM_CHUNK
    nc = S // L
    f32 = jnp.float32
    xdt = (x.astype(f32) * dt[..., None]).reshape(b, nc, L, G, hg, P)
    log_a = (dt * a).reshape(b, nc, L, G, hg)
    Bc = Bm.astype(f32).reshape(b, nc, L, G, N)
    Cc = Cm.astype(f32).reshape(b, nc, L, G, N)
    a_cum = jnp.cumsum(log_a, axis=2)
    causal = jnp.tril(jnp.ones((L, L), dtype=bool))[None, None, :, :, None, None]
    seg = a_cum[:, :, :, None] - a_cum[:, :, None, :]
    decay = jnp.where(causal, jnp.exp(jnp.where(causal, seg, 0.0)), 0.0)
    cb = jnp.einsum("bclgn,bcsgn->bclsg", Cc, Bc)
    y_diag = jnp.einsum("bclsg,bclsgh,bcsghp->bclghp", cb, decay, xdt)
    decay_to_end = jnp.exp(a_cum[:, :, -1:] - a_cum)
    chunk_states = jnp.einsum("bclgn,bclgh,bclghp->bcghpn", Bc, decay_to_end, xdt)
    chunk_decay = jnp.exp(a_cum[:, :, -1])

    def step(state, inp):
        cs, cd = inp
        return state * cd[..., None, None] + cs, state

    s0 = jnp.zeros((b, G, hg, P, N), f32)
    _, states_in = lax.scan(step, s0, (jnp.moveaxis(chunk_states, 1, 0), jnp.moveaxis(chunk_decay, 1, 0)))
    states_in = jnp.moveaxis(states_in, 0, 1)
    y_off = jnp.einsum("bclgn,bcghpn,bclgh->bclghp", Cc, states_in, jnp.exp(a_cum))
    return (y_diag + y_off).reshape(b, S, H, P)


def gated_delta_rule_chunked(q, k, v, g, beta):
    b, S, H, dk = q.shape
    dv = v.shape[-1]
    C = DN_CHUNK
    n = S // C
    f32 = jnp.float32

    def chunks(t):
        return jnp.moveaxis(t.astype(f32).reshape(b, n, C, *t.shape[2:]), 2, 3)

    q, k, v, g, beta = (chunks(t) for t in (q, k, v, g, beta))
    G = jnp.cumsum(g, axis=-1)
    incl = jnp.tril(jnp.ones((C, C), dtype=bool))
    strict = jnp.tril(jnp.ones((C, C), dtype=bool), -1)
    diff = G[..., :, None] - G[..., None, :]
    decay = jnp.where(incl, jnp.exp(jnp.where(incl, diff, 0.0)), 0.0)
    kb = k * beta[..., None]
    M = jnp.where(strict, jnp.einsum("bnhid,bnhjd->bnhij", kb, k) * decay, 0.0)
    eye = jnp.eye(C, dtype=f32)
    rhs = jnp.concatenate([v * beta[..., None], kb * jnp.exp(G)[..., None]], axis=-1)
    sol = lax.linalg.triangular_solve(eye + M, rhs, left_side=True, lower=True, unit_diagonal=True)
    u, w = sol[..., :dv], sol[..., dv:]
    a_qk = jnp.einsum("bnhid,bnhjd->bnhij", q, k) * decay
    q_dec = q * jnp.exp(G)[..., None]
    g_last = G[..., -1]
    k_dec = k * jnp.exp(g_last[..., None] - G)[..., None]

    def step(state, inp):
        q_c, a_c, u_c, w_c, k_c, gl = inp
        v_new = u_c - jnp.einsum("bhck,bhkv->bhcv", w_c, state)
        o = jnp.einsum("bhck,bhkv->bhcv", q_c, state) + jnp.einsum("bhij,bhjv->bhiv", a_c, v_new)
        state = state * jnp.exp(gl)[..., None, None] + jnp.einsum("bhck,bhcv->bhkv", k_c, v_new)
        return state, o

    xs = tuple(jnp.moveaxis(t, 1, 0) for t in (q_dec, a_qk, u, w, k_dec, g_last))
    s0 = jnp.zeros((b, H, dk, dv), f32)
    _, o = lax.scan(step, s0, xs)
    o = jnp.moveaxis(jnp.moveaxis(o, 0, 1), 2, 3)
    return o.reshape(b, S, H, dv)


def mamba2_branch(z, xbc, dt_raw, conv_w, conv_b, dt_bias, a_log, d_skip, norm_w):
    b, S, _ = xbc.shape
    f32 = jnp.float32
    xbc = jax.nn.silu(causal_depthwise_conv(xbc, conv_w, conv_b))
    xs, Bm, Cm = jnp.split(xbc, [SSM_INNER, SSM_INNER + SSM_GROUPS * SSM_STATE], axis=-1)
    xs = xs.reshape(b, S, SSM_HEADS, SSM_HEAD_DIM)
    Bm = Bm.reshape(b, S, SSM_GROUPS, SSM_STATE)
    Cm = Cm.reshape(b, S, SSM_GROUPS, SSM_STATE)
    dt = jax.nn.softplus(dt_raw.astype(f32) + dt_bias.astype(f32))
    a = -jnp.exp(a_log.astype(f32))
    y = ssd_chunked(xs, dt, a, Bm, Cm) + xs.astype(f32) * d_skip.astype(f32)[:, None]
    yg = (y.reshape(b, S, SSM_INNER) * jax.nn.silu(z.astype(f32))).reshape(b, S, SSM_GROUPS, -1)
    yg = yg * lax.rsqrt(jnp.mean(yg * yg, axis=-1, keepdims=True) + EPS)
    return (yg.reshape(b, S, SSM_INNER) * norm_w.astype(f32)).astype(z.dtype)


def deltanet_branch(qkv, z, b_raw, a_raw, conv_w, dt_bias, a_log, norm_w):
    b, S, _ = qkv.shape
    f32 = jnp.float32
    qkv = jax.nn.silu(causal_depthwise_conv(qkv, conv_w))
    q, k, v = jnp.split(qkv, [DN_K_DIM, 2 * DN_K_DIM], axis=-1)
    q = l2_norm(q.reshape(b, S, DN_HEADS, DN_HEAD_K)) * (DN_HEAD_K ** -0.5)
    k = l2_norm(k.reshape(b, S, DN_HEADS, DN_HEAD_K))
    v = v.reshape(b, S, DN_HEADS, DN_HEAD_V)
    beta = jax.nn.sigmoid(b_raw.astype(f32))
    g = -jnp.exp(a_log.astype(f32)) * jax.nn.softplus(a_raw.astype(f32) + dt_bias.astype(f32))
    o = gated_delta_rule_chunked(q, k, v, g, beta)
    o = o * lax.rsqrt(jnp.mean(o * o, axis=-1, keepdims=True) + EPS) * norm_w.astype(f32)
    o = o * jax.nn.silu(z.astype(f32).reshape(b, S, DN_HEADS, DN_HEAD_V))
    return o.reshape(b, S, DN_V_DIM).astype(z.dtype)


def setup_inputs(seed: int = 0) -> dict:
    key = jax.random.key(seed)
    k = jax.random.split(key, 23)
    f32 = jnp.float32

    def dense(kk, shape, fan_in):
        return jax.random.normal(kk, shape, f32) * fan_in ** -0.5

    def gain(kk, shape):
        return 1.0 + 0.02 * jax.random.normal(kk, shape, f32)

    def dt_bias(kk, shape):
        u = jax.random.uniform(kk, shape, f32)
        dt = jnp.exp(u * (math.log(0.1) - math.log(1e-3)) + math.log(1e-3))
        return dt + jnp.log(-jnp.expm1(-dt))

    def a_log(kk, shape):
        return jnp.log(jax.random.uniform(kk, shape, f32, 1.0, 16.0))

    L = DEPTH
    return {
        "x": jax.random.normal(k[0], (BATCH, SEQ, D_MODEL), f32),
        "ffn1_norm": gain(k[1], (L, D_MODEL)),
        "ffn1_w_in": dense(k[2], (L, D_MODEL, 2 * D_FF), D_MODEL),
        "ffn1_w_out": dense(k[3], (L, D_FF, D_MODEL), D_FF),
        "mix_norm": gain(k[4], (L, D_MODEL)),
        "w_in": dense(k[5], (L, D_MODEL, IN_DIM), D_MODEL),
        "ssm_conv_w": dense(k[6], (L, CONV_K, SSM_CONV_DIM), CONV_K),
        "ssm_conv_b": 0.02 * jax.random.normal(k[7], (L, SSM_CONV_DIM), f32),
        "ssm_dt_bias": dt_bias(k[8], (L, SSM_HEADS)),
        "ssm_a_log": a_log(k[9], (L, SSM_HEADS)),
        "ssm_d": 1.0 + 0.1 * jax.random.normal(k[10], (L, SSM_HEADS), f32),
        "ssm_norm": gain(k[11], (L, SSM_INNER)),
        "ssm_w_branch": dense(k[12], (L, SSM_INNER, D_MODEL), SSM_INNER),
        "dn_conv_w": dense(k[13], (L, CONV_K, DN_CONV_DIM), CONV_K),
        "dn_dt_bias": dt_bias(k[14], (L, DN_HEADS)),
        "dn_a_log": a_log(k[15], (L, DN_HEADS)),
        "dn_norm": gain(k[16], (L, DN_HEAD_V)),
        "dn_w_branch": dense(k[17], (L, DN_V_DIM, D_MODEL), DN_V_DIM),
        "w_out": dense(k[18], (L, D_MODEL, D_MODEL), D_MODEL),
        "ffn2_norm": gain(k[19], (L, D_MODEL)),
        "ffn2_w_in": dense(k[20], (L, D_MODEL, 2 * D_FF), D_MODEL),
        "ffn2_w_out": dense(k[21], (L, D_FF, D_MODEL), D_FF),
        "final_norm": gain(k[22], (D_MODEL,)),
    }


def reference(x, ffn1_norm, ffn1_w_in, ffn1_w_out, mix_norm, w_in, ssm_conv_w, ssm_conv_b,
              ssm_dt_bias, ssm_a_log, ssm_d, ssm_norm, ssm_w_branch, dn_conv_w, dn_dt_bias,
              dn_a_log, dn_norm, dn_w_branch, w_out, ffn2_norm, ffn2_w_in, ffn2_w_out, final_norm):
    split_points = [int(p) for p in np.cumsum(IN_SPLIT_SIZES)[:-1]]
    h = x
    for l in range(DEPTH):
        h = h + 0.5 * swiglu(rms_norm(h, ffn1_norm[l]), ffn1_w_in[l], ffn1_w_out[l])
        u = rms_norm(h, mix_norm[l])
        (z_s, xbc, dt_s, qkv, z_d, b_d, a_d, gate_s, gate_d) = jnp.split(u @ w_in[l], split_points, axis=-1)
        y_s = mamba2_branch(z_s, xbc, dt_s, ssm_conv_w[l], ssm_conv_b[l], ssm_dt_bias[l],
                            ssm_a_log[l], ssm_d[l], ssm_norm[l])
        y_d = deltanet_branch(qkv, z_d, b_d, a_d, dn_conv_w[l], dn_dt_bias[l],
                              dn_a_log[l], dn_norm[l])
        merged = (jax.nn.sigmoid(gate_s) * (y_s @ ssm_w_branch[l])
                  + jax.nn.sigmoid(gate_d) * (y_d @ dn_w_branch[l]))
        h = h + merged @ w_out[l]
        h = h + 0.5 * swiglu(rms_norm(h, ffn2_norm[l]), ffn2_w_in[l], ffn2_w_out[l])
    return rms_norm(h, final_norm)
```

```python
import functools

import jax
import jax.numpy as jnp
from jax import lax
from jax.experimental import pallas as pl
from jax.experimental.pallas import tpu as pltpu

F32 = jnp.float32
BF16 = jnp.bfloat16
HIGHEST = lax.Precision.HIGHEST

CONV_K = 4
SSM_HEAD_DIM = 64
SSM_GROUPS = 4
SSM_STATE = 128
SSM_CHUNK = 128
DN_HEADS = 8
DN_HEAD_K = 128
DN_HEAD_V = 128
DN_CHUNK = 64
EPS = 1e-6

LANES = 128
SUBLANES = 8
VMEM_LIMIT = 56 * 1024 * 1024

SM_DT = 0
SM_BETA = 32
SM_G = 40
SM_LA = 48

NT_DIMS = (((1,), (1,)), ((), ()))
TN_DIMS = (((0,), (0,)), ((), ()))


def _sigmoid(x):
    return 1.0 / (1.0 + jnp.exp(-x))


def _softplus(x):
    return jnp.maximum(x, 0.0) + jnp.log1p(jnp.exp(-jnp.abs(x)))


def _rms_norm(x, w):
    return x * lax.rsqrt(jnp.mean(x * x, axis=-1, keepdims=True) + EPS) * w


def _dot(a, b, **kw):
    return jnp.dot(a, b, preferred_element_type=F32, **kw)


def _params(*sem):
    return pltpu.CompilerParams(dimension_semantics=sem, vmem_limit_bytes=VMEM_LIMIT)


def _ffn_kernel(h_ref, nw_ref, win_ref, wout_ref, nw2_ref, *rest, d_ff, tf, mode):
    if mode == "emit_u":
        o_ref, u_ref, act_ref = rest
    else:
        o_ref, act_ref = rest
    h = h_ref[...]
    xn = _rms_norm(h, nw_ref[...]).astype(BF16)
    for c in range(d_ff // tf):
        g = _dot(xn, win_ref[:, c * tf:(c + 1) * tf])
        u = _dot(xn, win_ref[:, d_ff + c * tf:d_ff + (c + 1) * tf])
        act_ref[:, c * tf:(c + 1) * tf] = (g * _sigmoid(g) * u).astype(BF16)
    hn = h + 0.5 * _dot(act_ref[...], wout_ref[...])
    if mode == "emit_u":
        o_ref[...] = hn
        u_ref[...] = _rms_norm(hn, nw2_ref[...]).astype(BF16)
    elif mode == "final":
        o_ref[...] = _rms_norm(hn, nw2_ref[...])
    else:
        o_ref[...] = hn


def _ffn(h, nw, win_b, wout_b, nw2, layer, mode, tm=512, tf=256):
    S, D = h.shape
    d_ff = wout_b.shape[1]
    row = pl.BlockSpec((tm, D), lambda i: (i, 0))
    vec = lambda a: pl.BlockSpec((None, 1, D), lambda i: (layer, 0, 0))
    once = pl.Buffered(1)
    in_specs = [
        row,
        vec(nw),
        pl.BlockSpec((None, D, 2 * d_ff), lambda i: (layer, 0, 0), pipeline_mode=once),
        pl.BlockSpec((None, d_ff, D), lambda i: (layer, 0, 0), pipeline_mode=once),
        pl.BlockSpec((None, 1, D), lambda i: (nw2.shape[0] - 1 if mode == "final" else layer, 0, 0)),
    ]
    if mode == "emit_u":
        out_shape = (jax.ShapeDtypeStruct((S, D), F32), jax.ShapeDtypeStruct((S, D), BF16))
        out_specs = (row, row)
    else:
        out_shape = jax.ShapeDtypeStruct((S, D), F32)
        out_specs = row
    return pl.pallas_call(
        functools.partial(_ffn_kernel, d_ff=d_ff, tf=tf, mode=mode),
        out_shape=out_shape, grid=(S // tm,), in_specs=in_specs, out_specs=out_specs,
        scratch_shapes=[pltpu.VMEM((tm, d_ff), BF16)],
        compiler_params=_params("parallel"), name="ffn_" + mode,
    )(h, nw, win_b, wout_b, nw2)


PJ_TN = 1024
PJ_CONV_BLOCKS = 6
PJ_Q, PJ_K, PJ_V = 3, 4, 5
PJ_ZS, PJ_ZD, PJ_GS, PJ_GD = 3, 8, 9, 10
PJ_SILU_END = 9
PJ_BLOCKS = 11


def _proj_kernel(u_ref, w_ref, cw_ref, cb_ref, o_ref, ext_ref, *, tm):
    j = pl.program_id(0)
    i = pl.program_id(1)
    y = _dot(u_ref[...], w_ref[...])

    @pl.when(j < PJ_CONV_BLOCKS)
    def _conv():
        @pl.when(i == 0)
        def _():
            ext_ref[0:SUBLANES, :] = jnp.zeros((SUBLANES, PJ_TN), F32)

        ext_ref[SUBLANES:SUBLANES + tm, :] = y
        cw = cw_ref[...]
        acc = cb_ref[...] + cw[CONV_K - 1:CONV_K, :] * y
        for k in range(CONV_K - 1):
            off = SUBLANES - (CONV_K - 1) + k
            acc = acc + cw[k:k + 1, :] * ext_ref[off:off + tm, :]
        ext_ref[0:SUBLANES, :] = y[tm - SUBLANES:tm, :]
        act = acc * _sigmoid(acc)
        is_qk = jnp.logical_or(j == PJ_Q, j == PJ_K)

        @pl.when(is_qk)
        def _():
            scale = jnp.where(j == PJ_Q, DN_HEAD_K ** -0.5, 1.0).astype(F32)
            for g in range(PJ_TN // DN_HEAD_K):
                a = act[:, g * DN_HEAD_K:(g + 1) * DN_HEAD_K]
                ss = jnp.sum(a * a, axis=-1, keepdims=True)
                o_ref[:, g * DN_HEAD_K:(g + 1) * DN_HEAD_K] = a * (lax.rsqrt(ss + EPS) * scale)

        @pl.when(jnp.logical_not(is_qk))
        def _():
            o_ref[...] = act

    @pl.when(j >= PJ_CONV_BLOCKS)
    def _plain():
        s = _sigmoid(y)
        o_ref[...] = jnp.where(j < PJ_SILU_END, y * s, s)


def _proj(u, w_big, conv_w, conv_b, layer, tm=512):
    S, D = u.shape
    last_conv = PJ_CONV_BLOCKS - 1
    return pl.pallas_call(
        functools.partial(_proj_kernel, tm=tm),
        out_shape=jax.ShapeDtypeStruct((S, PJ_BLOCKS * PJ_TN), F32),
        grid=(PJ_BLOCKS, S // tm),
        in_specs=[
            pl.BlockSpec((tm, D), lambda j, i: (i, 0)),
            pl.BlockSpec((None, D, PJ_TN), lambda j, i: (layer, 0, j)),
            pl.BlockSpec((None, CONV_K, PJ_TN), lambda j, i: (layer, 0, jnp.minimum(j, last_conv))),
            pl.BlockSpec((None, 1, PJ_TN), lambda j, i: (layer, 0, jnp.minimum(j, last_conv))),
        ],
        out_specs=pl.BlockSpec((tm, PJ_TN), lambda j, i: (i, j)),
        scratch_shapes=[pltpu.VMEM((tm + SUBLANES, PJ_TN), F32)],
        compiler_params=_params("arbitrary", "arbitrary"), name="proj",
    )(u, w_big, conv_w, conv_b)


def _small_act(y, bias, a_log, kind):
    sp = _softplus(y + bias)
    return jnp.where(kind == 0.0, sp, jnp.where(kind == 1.0, _sigmoid(y), -jnp.exp(a_log) * sp))


def _small_kernel(u_ref, w_ref, wt_ref, prow_ref, pcol_ref, o_ref, ot_ref):
    u = u_ref[...]
    y = _dot(u, w_ref[...])
    yt = lax.dot_general(wt_ref[...], u, NT_DIMS, preferred_element_type=F32)
    o_ref[...] = _small_act(y, prow_ref[0:1, :], prow_ref[1:2, :], prow_ref[2:3, :])
    ot_ref[...] = _small_act(yt, pcol_ref[:, 0:1], pcol_ref[:, 1:2], pcol_ref[:, 2:3])


def _small_proj(u, w_sm, w_sm_t, prow, pcol, layer, tm=1024):
    S, D = u.shape
    return pl.pallas_call(
        _small_kernel,
        out_shape=(jax.ShapeDtypeStruct((S, LANES), F32), jax.ShapeDtypeStruct((LANES, S), F32)),
        grid=(S // tm,),
        in_specs=[
            pl.BlockSpec((tm, D), lambda i: (i, 0)),
            pl.BlockSpec((None, D, LANES), lambda i: (layer, 0, 0)),
            pl.BlockSpec((None, LANES, D), lambda i: (layer, 0, 0)),
            pl.BlockSpec((None, SUBLANES, LANES), lambda i: (layer, 0, 0)),
            pl.BlockSpec((None, LANES, SUBLANES), lambda i: (layer, 0, 0)),
        ],
        out_specs=(pl.BlockSpec((tm, LANES), lambda i: (i, 0)),
                   pl.BlockSpec((LANES, tm), lambda i: (0, i))),
        compiler_params=_params("parallel"), name="small_proj",
    )(u, w_sm, w_sm_t, prow, pcol)


def _split2(a):
    hi = a.astype(BF16)
    lo = (a - hi.astype(F32)).astype(BF16)
    return jnp.concatenate([hi, lo], axis=1)


def _ssd_kernel(x_ref, b_ref, c_ref, z_ref, sm_ref, smt_ref, edt_ref, ela_ref, dskip_ref,
                nw_ref, o_ref, state_ref, yg_ref, *, n_heads):
    L = SSM_CHUNK
    P = SSM_HEAD_DIM
    N = SSM_STATE
    hg = n_heads // SSM_GROUPS
    gw = hg * P

    @pl.when(pl.program_id(0) == 0)
    def _():
        state_ref[...] = jnp.zeros_like(state_ref)

    row = lax.broadcasted_iota(jnp.int32, (L, L), 0)
    col = lax.broadcasted_iota(jnp.int32, (L, L), 1)
    causal = row >= col
    tri = causal.astype(F32)
    tri_t = (row <= col).astype(F32)
    la_lane = jnp.logical_and(col >= SM_LA, col < SM_LA + n_heads)
    la_row = jnp.logical_and(row >= SM_LA, row < SM_LA + n_heads)

    sm = sm_ref[...]
    acum = _dot(tri, jnp.where(la_lane, sm, 0.0), precision=HIGHEST)
    acum_t = _dot(jnp.where(la_row, smt_ref[...], 0.0), tri_t, precision=HIGHEST)
    eacum = jnp.exp(acum)
    dte = jnp.exp(acum[L - 1:L, :] - acum)
    e_dt = _dot(_split2(sm), edt_ref[...])
    e2 = _dot(_split2(jnp.concatenate([dte, eacum], axis=0)), ela_ref[...])
    e_dte = e2[:L]
    e_ea = e2[L:]

    x = x_ref[...]
    xdt = x * e_dt
    xsc_b = (xdt * e_dte).astype(BF16)
    low_half = col < P
    bm = b_ref[...].astype(BF16)
    cm = c_ref[...].astype(BF16)
    dskip = dskip_ref[...]
    nw = nw_ref[...]
    for g in range(SSM_GROUPS):
        gs = slice(g * gw, (g + 1) * gw)
        bg = bm[:, g * N:(g + 1) * N]
        cg = cm[:, g * N:(g + 1) * N]
        cb = lax.dot_general(cg, bg, NT_DIMS, preferred_element_type=F32)
        st = state_ref[:, gs]
        yoff = _dot(cg, st.astype(BF16)) * e_ea[:, gs]
        state_ref[:, gs] = st * e_ea[L - 1:L, gs] + lax.dot_general(
            bg, xsc_b[:, gs], TN_DIMS, preferred_element_type=F32)
        for pp in range(hg // 2):
            h0 = g * hg + 2 * pp
            ws = []
            for hh in (h0, h0 + 1):
                seg = acum[:, SM_LA + hh:SM_LA + hh + 1] - acum_t[SM_LA + hh:SM_LA + hh + 1, :]
                dec = jnp.where(causal, jnp.exp(jnp.where(causal, seg, 0.0)), 0.0)
                ws.append((cb * dec).astype(BF16))
            lhs = jnp.concatenate(ws, axis=1)
            cs = slice(h0 * P, (h0 + 2) * P)
            xp = xdt[:, cs]
            rhs = jnp.concatenate([jnp.where(low_half, xp, 0.0), jnp.where(low_half, 0.0, xp)],
                                  axis=0).astype(BF16)
            y = _dot(lhs, rhs) + yoff[:, pp * 2 * P:(pp + 1) * 2 * P] + x[:, cs] * dskip[:, cs]
            yg_ref[:, cs] = y * z_ref[:, cs]
        yg = yg_ref[:, gs]
        ms = jnp.sum(yg * yg, axis=-1, keepdims=True) * (1.0 / gw)
        o_ref[:, gs] = (yg * lax.rsqrt(ms + EPS) * nw[:, gs]).astype(BF16)


def _ssd(big, small, small_t, e_dt, e_la, dskip, nw, layer, n_heads):
    S = big.shape[0]
    L = SSM_CHUNK
    inner = n_heads * SSM_HEAD_DIM
    gn = SSM_GROUPS * SSM_STATE
    const = lambda c: (0, 0)
    return pl.pallas_call(
        functools.partial(_ssd_kernel, n_heads=n_heads),
        out_shape=jax.ShapeDtypeStruct((S, inner), BF16),
        grid=(S // L,),
        in_specs=[
            pl.BlockSpec((L, inner), lambda c: (c, 0)),
            pl.BlockSpec((L, gn), lambda c: (c, inner // gn)),
            pl.BlockSpec((L, gn), lambda c: (c, inner // gn + 1)),
            pl.BlockSpec((L, inner), lambda c: (c, PJ_ZS)),
            pl.BlockSpec((L, LANES), lambda c: (c, 0)),
            pl.BlockSpec((LANES, L), lambda c: (0, c)),
            pl.BlockSpec((2 * LANES, inner), const),
            pl.BlockSpec((2 * LANES, inner), const),
            pl.BlockSpec((None, 1, inner), lambda c: (layer, 0, 0)),
            pl.BlockSpec((None, 1, inner), lambda c: (layer, 0, 0)),
        ],
        out_specs=pl.BlockSpec((L, inner), lambda c: (c, 0)),
        scratch_shapes=[pltpu.VMEM((SSM_STATE, inner), F32), pltpu.VMEM((L, inner), F32)],
        compiler_params=_params("arbitrary"), name="ssd",
    )(big, big, big, big, small, small_t, e_dt, e_la, dskip, nw)


DN_BLOCK = 16


def _unit_lower_inverse(m, eye, blk):
    dot = functools.partial(_dot, precision=HIGHEST)
    n = jnp.where(blk, m, 0.0)
    off = jnp.where(blk, 0.0, m)
    x = eye - n
    p = dot(n, n)
    steps = DN_BLOCK.bit_length() - 2
    for s in range(steps):
        x = x + dot(x, p)
        if s + 1 < steps:
            p = dot(p, p)
    q = dot(x, off)
    q2 = dot(q, q)
    r = eye - q + q2 - dot(q, q2)
    return dot(r, x)


def _gdn_kernel(q_ref, k_ref, v_ref, z_ref, sm_ref, smt_ref, nw_ref, o_ref, state_ref, *, sub):
    C = DN_CHUNK
    assert C // DN_BLOCK == 4

    @pl.when(pl.program_id(0) == 0)
    def _():
        state_ref[...] = jnp.zeros_like(state_ref)

    row = lax.broadcasted_iota(jnp.int32, (C, C), 0)
    col = lax.broadcasted_iota(jnp.int32, (C, C), 1)
    incl = row >= col
    strict = row > col
    eye = (row == col).astype(F32)
    shift = DN_BLOCK.bit_length() - 1
    blk = (row >> shift) == (col >> shift)
    tri = incl.astype(F32)
    tri_t = (row <= col).astype(F32)
    lane = lax.broadcasted_iota(jnp.int32, (C, LANES), 1)
    rowi = lax.broadcasted_iota(jnp.int32, (LANES, C), 0)
    g_lane = jnp.logical_and(lane >= SM_G, lane < SM_G + DN_HEADS)
    g_row = jnp.logical_and(rowi >= SM_G, rowi < SM_G + DN_HEADS)
    nw = nw_ref[...]

    for s in range(sub):
        rs = slice(s * C, (s + 1) * C)
        sm = sm_ref[rs, :]
        gcum = _dot(tri, jnp.where(g_lane, sm, 0.0), precision=HIGHEST)
        gcum_t = _dot(jnp.where(g_row, smt_ref[:, rs], 0.0), tri_t, precision=HIGHEST)
        eg = jnp.exp(gcum)
        glast = gcum[C - 1:C, :]
        ekd = jnp.exp(glast - gcum)
        eglast = jnp.exp(glast)
        for h in range(DN_HEADS):
            hs = slice(h * DN_HEAD_K, (h + 1) * DN_HEAD_K)
            gl = slice(SM_G + h, SM_G + h + 1)
            qh = q_ref[rs, hs]
            kh = k_ref[rs, hs]
            vh = v_ref[rs, hs]
            beta = sm[:, SM_BETA + h:SM_BETA + h + 1]
            diff = gcum[:, gl] - gcum_t[gl, :]
            dec = jnp.where(incl, jnp.exp(jnp.where(incl, diff, 0.0)), 0.0)
            kb = kh * beta
            k_b = kh.astype(BF16)
            kk = lax.dot_general(kb.astype(BF16), k_b, NT_DIMS, preferred_element_type=F32)
            m = jnp.where(strict, kk * dec, 0.0)
            tinv = _unit_lower_inverse(m, eye, blk)
            rhs = jnp.concatenate([vh * beta, kb * eg[:, gl]], axis=1)
            sol = _dot(tinv, rhs, precision=HIGHEST)
            u = sol[:, :DN_HEAD_V]
            w = sol[:, DN_HEAD_V:]
            aqk = lax.dot_general(qh.astype(BF16), k_b, NT_DIMS, preferred_element_type=F32) * dec
            qd = (qh * eg[:, gl]).astype(BF16)
            kd = (kh * ekd[:, gl]).astype(BF16)
            st = state_ref[h]
            st_b = st.astype(BF16)
            v_new = u - _dot(w.astype(BF16), st_b)
            v_b = v_new.astype(BF16)
            o = _dot(qd, st_b) + _dot(aqk.astype(BF16), v_b)
            state_ref[h] = st * eglast[:, gl] + lax.dot_general(
                kd, v_b, TN_DIMS, preferred_element_type=F32)
            ms = jnp.mean(o * o, axis=-1, keepdims=True)
            o_ref[rs, hs] = (o * lax.rsqrt(ms + EPS) * nw * z_ref[rs, hs]).astype(BF16)


def _gdn(big, small, small_t, nw, layer, sub=2):
    S = big.shape[0]
    R = sub * DN_CHUNK
    W = DN_HEADS * DN_HEAD_K
    assert R == LANES and W == PJ_TN
    return pl.pallas_call(
        functools.partial(_gdn_kernel, sub=sub),
        out_shape=jax.ShapeDtypeStruct((S, DN_HEADS * DN_HEAD_V), BF16),
        grid=(S // R,),
        in_specs=[
            pl.BlockSpec((R, W), lambda c: (c, PJ_Q)),
            pl.BlockSpec((R, W), lambda c: (c, PJ_K)),
            pl.BlockSpec((R, W), lambda c: (c, PJ_V)),
            pl.BlockSpec((R, W), lambda c: (c, PJ_ZD)),
            pl.BlockSpec((R, LANES), lambda c: (c, 0)),
            pl.BlockSpec((LANES, R), lambda c: (0, c)),
            pl.BlockSpec((None, 1, DN_HEAD_V), lambda c: (layer, 0, 0)),
        ],
        out_specs=pl.BlockSpec((R, W), lambda c: (c, 0)),
        scratch_shapes=[pltpu.VMEM((DN_HEADS, DN_HEAD_K, DN_HEAD_V), F32)],
        compiler_params=_params("arbitrary"), name="gdn",
    )(big, big, big, big, small, small_t, nw)


def _mixout_kernel(h_ref, ys_ref, yd_ref, gs_ref, gd_ref, ws_ref, wd_ref, wo_ref, o_ref):
    merged = gs_ref[...] * _dot(ys_ref[...], ws_ref[...]) + gd_ref[...] * _dot(yd_ref[...], wd_ref[...])
    o_ref[...] = h_ref[...] + _dot(merged.astype(BF16), wo_ref[...])


def _mixout(h, ys, yd, big, ws_b, wd_b, wo_b, layer, tm=512):
    S, D = h.shape
    assert D == PJ_TN
    row = lambda w: pl.BlockSpec((tm, w), lambda i: (i, 0))
    wspec = lambda a: pl.BlockSpec((None,) + a.shape[1:], lambda i: (layer, 0, 0))
    return pl.pallas_call(
        _mixout_kernel,
        out_shape=jax.ShapeDtypeStruct((S, D), F32),
        grid=(S // tm,),
        in_specs=[row(D), row(ys.shape[1]), row(yd.shape[1]),
                  pl.BlockSpec((tm, D), lambda i: (i, PJ_GS)),
                  pl.BlockSpec((tm, D), lambda i: (i, PJ_GD)),
                  wspec(ws_b), wspec(wd_b), wspec(wo_b)],
        out_specs=row(D),
        compiler_params=_params("parallel"), name="mixout",
    )(h, ys, yd, big, big, ws_b, wd_b, wo_b)


def kernel(x, ffn1_norm, ffn1_w_in, ffn1_w_out, mix_norm, w_in, ssm_conv_w, ssm_conv_b, ssm_dt_bias, ssm_a_log, ssm_d, ssm_norm, ssm_w_branch, dn_conv_w, dn_dt_bias, dn_a_log, dn_norm, dn_w_branch, w_out, ffn2_norm, ffn2_w_in, ffn2_w_out, final_norm):
    B, S, D = x.shape
    assert B == 1
    depth = w_in.shape[0]
    H = ssm_dt_bias.shape[1]
    inner = H * SSM_HEAD_DIM
    gn = SSM_GROUPS * SSM_STATE
    dn_k = DN_HEADS * DN_HEAD_K
    dn_v = DN_HEADS * DN_HEAD_V
    assert ssm_norm.shape[1] == inner and ssm_conv_w.shape[2] == inner + 2 * gn
    assert dn_conv_w.shape[2] == 2 * dn_k + dn_v and D == PJ_TN and dn_k == PJ_TN and dn_v == PJ_TN
    assert inner == 2 * PJ_TN and 2 * gn == PJ_TN and SM_LA + H <= LANES

    sizes = (inner, inner + 2 * gn, H, 2 * dn_k + dn_v, dn_v, DN_HEADS, DN_HEADS, D, D)
    offs = [0]
    for sz in sizes:
        offs.append(offs[-1] + sz)
    assert offs[-1] == w_in.shape[2]
    sl = lambda i: slice(offs[i], offs[i + 1])
    z_s, xbc, dt, qkv, z_d, b_d, a_d, g_s, g_d = (w_in[:, :, sl(i)] for i in range(9))

    w_big = jnp.concatenate([xbc, qkv, z_s, z_d, g_s, g_d], axis=-1).astype(BF16)
    pad = jnp.zeros((depth, D, LANES - (SM_LA + H)), F32)
    w_sm = jnp.concatenate([dt, b_d, a_d, dt, pad], axis=-1).astype(BF16)
    w_sm_t = jnp.swapaxes(w_sm, 1, 2)
    zeros_h = jnp.zeros((depth, DN_HEADS), F32)
    tail = jnp.zeros((depth, LANES - (SM_LA + H)), F32)
    bias = jnp.concatenate([ssm_dt_bias, zeros_h, dn_dt_bias, ssm_dt_bias, tail], axis=-1)
    a_log = jnp.concatenate([jnp.zeros_like(ssm_a_log), zeros_h, dn_a_log, ssm_a_log, tail], axis=-1)
    kind = jnp.concatenate([jnp.zeros((H,), F32), jnp.ones((DN_HEADS,), F32),
                            jnp.full((DN_HEADS + H,), 2.0, F32), jnp.ones((LANES - (SM_LA + H),), F32)])
    kind = jnp.broadcast_to(kind, (depth, LANES))
    prow = jnp.stack([bias, a_log, kind] + [jnp.zeros_like(bias)] * (SUBLANES - 3), axis=1)
    pcol = jnp.swapaxes(prow, 1, 2)
    conv_w = jnp.concatenate([ssm_conv_w, dn_conv_w], axis=-1)
    conv_b = jnp.concatenate([ssm_conv_b, jnp.zeros((depth, 2 * dn_k + dn_v), F32)], axis=-1)[:, None, :]

    head_of_col = jnp.arange(inner) // SSM_HEAD_DIM
    lane_id = jnp.arange(2 * LANES) % LANES
    e_dt = (lane_id[:, None] == SM_DT + head_of_col[None, :]).astype(BF16)
    e_la = (lane_id[:, None] == SM_LA + head_of_col[None, :]).astype(BF16)
    dskip = jnp.repeat(ssm_d, SSM_HEAD_DIM, axis=-1)[:, None, :]

    v3 = lambda a: a[:, None, :]
    f1_in, f1_out = ffn1_w_in.astype(BF16), ffn1_w_out.astype(BF16)
    f2_in, f2_out = ffn2_w_in.astype(BF16), ffn2_w_out.astype(BF16)
    ws_b, wd_b, wo_b = ssm_w_branch.astype(BF16), dn_w_branch.astype(BF16), w_out.astype(BF16)
    n1, nm, n2, nf = v3(ffn1_norm), v3(mix_norm), v3(ffn2_norm), final_norm[None, None, :]
    ns, nd = v3(ssm_norm), v3(dn_norm)

    h = x.reshape(S, D)
    for l in range(depth):
        h, u = _ffn(h, n1, f1_in, f1_out, nm, l, "emit_u")
        big = _proj(u, w_big, conv_w, conv_b, l)
        small, small_t = _small_proj(u, w_sm, w_sm_t, prow, pcol, l)
        ys = _ssd(big, small, small_t, e_dt, e_la, dskip, ns, l, H)
        yd = _gdn(big, small, small_t, nd, l)
        h = _mixout(h, ys, yd, big, ws_b, wd_b, wo_b, l)
        last = l == depth - 1
        h = _ffn(h, n2, f2_in, f2_out, nf if last else n2, l, "final" if last else "plain")
    return h.reshape(B, S, D)
```

```python
import functools

import jax
import jax.numpy as jnp
from jax import lax
from jax.experimental import pallas as pl
from jax.experimental.pallas import tpu as pltpu

F32 = jnp.float32
BF16 = jnp.bfloat16
HIGHEST = lax.Precision.HIGHEST

CONV_K = 4
SSM_HEAD_DIM = 64
SSM_GROUPS = 4
SSM_STATE = 128
SSM_CHUNK = 128
DN_HEADS = 8
DN_HEAD_K = 128
DN_HEAD_V = 128
DN_CHUNK = 64
EPS = 1e-6

LANES = 128
SUBLANES = 8
VMEM_LIMIT = 56 * 1024 * 1024

SM_DT = 0
SM_BETA = 32
SM_G = 40
SM_LA = 48

NT_DIMS = (((1,), (1,)), ((), ()))
TN_DIMS = (((0,), (0,)), ((), ()))


def _sigmoid(x):
    return 1.0 / (1.0 + jnp.exp(-x))


def _softplus(x):
    return jnp.maximum(x, 0.0) + jnp.log1p(jnp.exp(-jnp.abs(x)))


def _rms_norm(x, w):
    return x * lax.rsqrt(jnp.mean(x * x, axis=-1, keepdims=True) + EPS) * w


def _dot(a, b, **kw):
    return jnp.dot(a, b, preferred_element_type=F32, **kw)


def _params(*sem):
    return pltpu.CompilerParams(dimension_semantics=sem, vmem_limit_bytes=VMEM_LIMIT)


def _ffn_kernel(h_ref, nw_ref, win_ref, wout_ref, nw2_ref, *rest, d_ff, tf, mode):
    if mode == "emit_u":
        o_ref, u_ref, act_ref = rest
    else:
        o_ref, act_ref = rest
    h = h_ref[...]
    xn = _rms_norm(h, nw_ref[...]).astype(BF16)
    for c in range(d_ff // tf):
        g = _dot(xn, win_ref[:, c * tf:(c + 1) * tf])
        u = _dot(xn, win_ref[:, d_ff + c * tf:d_ff + (c + 1) * tf])
        act_ref[:, c * tf:(c + 1) * tf] = (g * _sigmoid(g) * u).astype(BF16)
    hn = h + 0.5 * _dot(act_ref[...], wout_ref[...])
    if mode == "emit_u":
        o_ref[...] = hn
        u_ref[...] = _rms_norm(hn, nw2_ref[...]).astype(BF16)
    elif mode == "final":
        o_ref[...] = _rms_norm(hn, nw2_ref[...])
    else:
        o_ref[...] = hn


def _ffn(h, nw, win_b, wout_b, nw2, layer, mode, tm=512, tf=256):
    S, D = h.shape
    d_ff = wout_b.shape[1]
    row = pl.BlockSpec((tm, D), lambda i: (i, 0))
    vec = lambda a: pl.BlockSpec((None, 1, D), lambda i: (layer, 0, 0))
    once = pl.Buffered(1)
    in_specs = [
        row,
        vec(nw),
        pl.BlockSpec((None, D, 2 * d_ff), lambda i: (layer, 0, 0), pipeline_mode=once),
        pl.BlockSpec((None, d_ff, D), lambda i: (layer, 0, 0), pipeline_mode=once),
        pl.BlockSpec((None, 1, D), lambda i: (nw2.shape[0] - 1 if mode == "final" else layer, 0, 0)),
    ]
    if mode == "emit_u":
        out_shape = (jax.ShapeDtypeStruct((S, D), F32), jax.ShapeDtypeStruct((S, D), BF16))
        out_specs = (row, row)
    else:
        out_shape = jax.ShapeDtypeStruct((S, D), F32)
        out_specs = row
    return pl.pallas_call(
        functools.partial(_ffn_kernel, d_ff=d_ff, tf=tf, mode=mode),
        out_shape=out_shape, grid=(S // tm,), in_specs=in_specs, out_specs=out_specs,
        scratch_shapes=[pltpu.VMEM((tm, d_ff), BF16)],
        compiler_params=_params("parallel"), name="ffn_" + mode,
    )(h, nw, win_b, wout_b, nw2)


PJ_TN = 1024
PJ_SUB = 256
PJ_CONV_BLOCKS = 6
PJ_Q, PJ_K, PJ_V = 3, 4, 5
PJ_ZS, PJ_ZD, PJ_GS, PJ_GD = 3, 8, 9, 10
PJ_SILU_END = 9
PJ_BLOCKS = 11


def _proj_kernel(u_ref, w_ref, cw_ref, cb_ref, o_ref, ext_ref, *, tm):
    j = pl.program_id(0)
    i = pl.program_id(1)
    is_conv = j < PJ_CONV_BLOCKS
    is_qk = jnp.logical_or(j == PJ_Q, j == PJ_K)
    chunks = [slice(c * PJ_SUB, (c + 1) * PJ_SUB) for c in range(PJ_TN // PJ_SUB)]

    @pl.when(jnp.logical_and(is_conv, i == 0))
    def _():
        ext_ref[0:SUBLANES, :] = jnp.zeros((SUBLANES, PJ_TN), F32)

    def conv_silu(cs):
        y = _dot(u_ref[...], w_ref[:, cs])
        ext_ref[SUBLANES:SUBLANES + tm, cs] = y
        acc = cb_ref[:, cs] + cw_ref[CONV_K - 1:CONV_K, cs] * y
        for k in range(CONV_K - 1):
            off = SUBLANES - (CONV_K - 1) + k
            acc = acc + cw_ref[k:k + 1, cs] * ext_ref[off:off + tm, cs]
        ext_ref[0:SUBLANES, cs] = y[tm - SUBLANES:tm, :]
        return acc * _sigmoid(acc)

    @pl.when(jnp.logical_and(is_conv, jnp.logical_not(is_qk)))
    def _():
        for cs in chunks:
            o_ref[:, cs] = conv_silu(cs)

    @pl.when(is_qk)
    def _():
        scale = jnp.where(j == PJ_Q, DN_HEAD_K ** -0.5, 1.0).astype(F32)
        for cs in chunks:
            act = conv_silu(cs)
            for g in range(PJ_SUB // DN_HEAD_K):
                a = act[:, g * DN_HEAD_K:(g + 1) * DN_HEAD_K]
                ss = jnp.sum(a * a, axis=-1, keepdims=True)
                lo = cs.start + g * DN_HEAD_K
                o_ref[:, lo:lo + DN_HEAD_K] = a * (lax.rsqrt(ss + EPS) * scale)

    @pl.when(jnp.logical_not(is_conv))
    def _():
        for cs in chunks:
            y = _dot(u_ref[...], w_ref[:, cs])
            s = _sigmoid(y)
            o_ref[:, cs] = jnp.where(j < PJ_SILU_END, y * s, s)


def _proj(u, w_big, conv_w, conv_b, layer, tm=512):
    S, D = u.shape
    last_conv = PJ_CONV_BLOCKS - 1
    return pl.pallas_call(
        functools.partial(_proj_kernel, tm=tm),
        out_shape=jax.ShapeDtypeStruct((S, PJ_BLOCKS * PJ_TN), F32),
        grid=(PJ_BLOCKS, S // tm),
        in_specs=[
            pl.BlockSpec((tm, D), lambda j, i: (i, 0)),
            pl.BlockSpec((None, D, PJ_TN), lambda j, i: (layer, 0, j)),
            pl.BlockSpec((None, CONV_K, PJ_TN), lambda j, i: (layer, 0, jnp.minimum(j, last_conv))),
            pl.BlockSpec((None, 1, PJ_TN), lambda j, i: (layer, 0, jnp.minimum(j, last_conv))),
        ],
        out_specs=pl.BlockSpec((tm, PJ_TN), lambda j, i: (i, j)),
        scratch_shapes=[pltpu.VMEM((tm + SUBLANES, PJ_TN), F32)],
        compiler_params=_params("arbitrary", "arbitrary"), name="proj",
    )(u, w_big, conv_w, conv_b)


def _small_act(y, bias, a_log, kind):
    sp = _softplus(y + bias)
    return jnp.where(kind == 0.0, sp, jnp.where(kind == 1.0, _sigmoid(y), -jnp.exp(a_log) * sp))


def _small_kernel(u_ref, w_ref, wt_ref, prow_ref, pcol_ref, o_ref, ot_ref):
    u = u_ref[...]
    y = _dot(u, w_ref[...])
    yt = lax.dot_general(wt_ref[...], u, NT_DIMS, preferred_element_type=F32)
    o_ref[...] = _small_act(y, prow_ref[0:1, :], prow_ref[1:2, :], prow_ref[2:3, :])
    ot_ref[...] = _small_act(yt, pcol_ref[:, 0:1], pcol_ref[:, 1:2], pcol_ref[:, 2:3])


def _small_proj(u, w_sm, w_sm_t, prow, pcol, layer, tm=1024):
    S, D = u.shape
    return pl.pallas_call(
        _small_kernel,
        out_shape=(jax.ShapeDtypeStruct((S, LANES), F32), jax.ShapeDtypeStruct((LANES, S), F32)),
        grid=(S // tm,),
        in_specs=[
            pl.BlockSpec((tm, D), lambda i: (i, 0)),
            pl.BlockSpec((None, D, LANES), lambda i: (layer, 0, 0)),
            pl.BlockSpec((None, LANES, D), lambda i: (layer, 0, 0)),
            pl.BlockSpec((None, SUBLANES, LANES), lambda i: (layer, 0, 0)),
            pl.BlockSpec((None, LANES, SUBLANES), lambda i: (layer, 0, 0)),
        ],
        out_specs=(pl.BlockSpec((tm, LANES), lambda i: (i, 0)),
                   pl.BlockSpec((LANES, tm), lambda i: (0, i))),
        compiler_params=_params("parallel"), name="small_proj",
    )(u, w_sm, w_sm_t, prow, pcol)


def _split2(a):
    hi = a.astype(BF16)
    lo = (a - hi.astype(F32)).astype(BF16)
    return jnp.concatenate([hi, lo], axis=1)


def _ssd_kernel(x_ref, b_ref, c_ref, z_ref, sm_ref, smt_ref, edt_ref, ela_ref, dskip_ref,
                nw_ref, o_ref, state_ref, yg_ref, *, n_heads):
    L = SSM_CHUNK
    P = SSM_HEAD_DIM
    N = SSM_STATE
    hg = n_heads // SSM_GROUPS
    gw = hg * P

    @pl.when(pl.program_id(0) == 0)
    def _():
        state_ref[...] = jnp.zeros_like(state_ref)

    row = lax.broadcasted_iota(jnp.int32, (L, L), 0)
    col = lax.broadcasted_iota(jnp.int32, (L, L), 1)
    causal = row >= col
    tri = causal.astype(F32)
    tri_t = (row <= col).astype(F32)
    la_lane = jnp.logical_and(col >= SM_LA, col < SM_LA + n_heads)
    la_row = jnp.logical_and(row >= SM_LA, row < SM_LA + n_heads)

    sm = sm_ref[...]
    acum = _dot(tri, jnp.where(la_lane, sm, 0.0), precision=HIGHEST)
    acum_t = _dot(jnp.where(la_row, smt_ref[...], 0.0), tri_t, precision=HIGHEST)
    eacum = jnp.exp(acum)
    dte = jnp.exp(acum[L - 1:L, :] - acum)
    e_dt = _dot(_split2(sm), edt_ref[...])
    e2 = _dot(_split2(jnp.concatenate([dte, eacum], axis=0)), ela_ref[...])
    e_dte = e2[:L]
    e_ea = e2[L:]

    x = x_ref[...]
    xdt = x * e_dt
    xsc_b = (xdt * e_dte).astype(BF16)
    low_half = col < P
    bm = b_ref[...].astype(BF16)
    cm = c_ref[...].astype(BF16)
    dskip = dskip_ref[...]
    nw = nw_ref[...]
    for g in range(SSM_GROUPS):
        gs = slice(g * gw, (g + 1) * gw)
        bg = bm[:, g * N:(g + 1) * N]
        cg = cm[:, g * N:(g + 1) * N]
        cb = lax.dot_general(cg, bg, NT_DIMS, preferred_element_type=F32)
        st = state_ref[:, gs]
        yoff = _dot(cg, st.astype(BF16)) * e_ea[:, gs]
        state_ref[:, gs] = st * e_ea[L - 1:L, gs] + lax.dot_general(
            bg, xsc_b[:, gs], TN_DIMS, preferred_element_type=F32)
        for pp in range(hg // 2):
            h0 = g * hg + 2 * pp
            ws = []
            for hh in (h0, h0 + 1):
                seg = acum[:, SM_LA + hh:SM_LA + hh + 1] - acum_t[SM_LA + hh:SM_LA + hh + 1, :]
                dec = jnp.where(causal, jnp.exp(jnp.where(causal, seg, 0.0)), 0.0)
                ws.append((cb * dec).astype(BF16))
            lhs = jnp.concatenate(ws, axis=1)
            cs = slice(h0 * P, (h0 + 2) * P)
            xp = xdt[:, cs]
            rhs = jnp.concatenate([jnp.where(low_half, xp, 0.0), jnp.where(low_half, 0.0, xp)],
                                  axis=0).astype(BF16)
            y = _dot(lhs, rhs) + yoff[:, pp * 2 * P:(pp + 1) * 2 * P] + x[:, cs] * dskip[:, cs]
            yg_ref[:, cs] = y * z_ref[:, cs]
        yg = yg_ref[:, gs]
        ms = jnp.sum(yg * yg, axis=-1, keepdims=True) * (1.0 / gw)
        o_ref[:, gs] = (yg * lax.rsqrt(ms + EPS) * nw[:, gs]).astype(BF16)


def _ssd(big, small, small_t, e_dt, e_la, dskip, nw, layer, n_heads):
    S = big.shape[0]
    L = SSM_CHUNK
    inner = n_heads * SSM_HEAD_DIM
    gn = SSM_GROUPS * SSM_STATE
    const = lambda c: (0, 0)
    return pl.pallas_call(
        functools.partial(_ssd_kernel, n_heads=n_heads),
        out_shape=jax.ShapeDtypeStruct((S, inner), BF16),
        grid=(S // L,),
        in_specs=[
            pl.BlockSpec((L, inner), lambda c: (c, 0)),
            pl.BlockSpec((L, gn), lambda c: (c, inner // gn)),
            pl.BlockSpec((L, gn), lambda c: (c, inner // gn + 1)),
            pl.BlockSpec((L, inner), lambda c: (c, PJ_ZS)),
            pl.BlockSpec((L, LANES), lambda c: (c, 0)),
            pl.BlockSpec((LANES, L), lambda c: (0, c)),
            pl.BlockSpec((2 * LANES, inner), const),
            pl.BlockSpec((2 * LANES, inner), const),
            pl.BlockSpec((None, 1, inner), lambda c: (layer, 0, 0)),
            pl.BlockSpec((None, 1, inner), lambda c: (layer, 0, 0)),
        ],
        out_specs=pl.BlockSpec((L, inner), lambda c: (c, 0)),
        scratch_shapes=[pltpu.VMEM((SSM_STATE, inner), F32), pltpu.VMEM((L, inner), F32)],
        compiler_params=_params("arbitrary"), name="ssd",
    )(big, big, big, big, small, small_t, e_dt, e_la, dskip, nw)


DN_BLOCK = 16


def _split(a):
    hi = a.astype(BF16)
    return hi, (a - hi.astype(F32)).astype(BF16)


def _mm3(a2, b2):
    (ah, al), (bh, bl) = a2, b2
    return _dot(ah, bh) + (_dot(ah, bl) + _dot(al, bh))


def _split3(g):
    g1 = g.astype(BF16)
    r1 = g - g1.astype(F32)
    g2 = r1.astype(BF16)
    return g1, g2, (r1 - g2.astype(F32)).astype(BF16)


def _cumsum_rows(tri_b, g):
    g1, g2, g3 = _split3(g)
    return _dot(tri_b, g1) + (_dot(tri_b, g2) + _dot(tri_b, g3))


def _cumsum_cols(g, tri_t_b):
    g1, g2, g3 = _split3(g)
    return _dot(g1, tri_t_b) + (_dot(g2, tri_t_b) + _dot(g3, tri_t_b))


def _unit_lower_solve(ms, rhss, eye, blk):
    ns = [jnp.where(blk, m, 0.0) for m in ms]
    offs = [_split(jnp.where(blk, 0.0, m)) for m in ms]
    xs = [eye - n for n in ns]
    ps = [_split(n) for n in ns]
    steps = DN_BLOCK.bit_length() - 1
    for s in range(1, steps):
        ps = [_split(_mm3(p, p)) for p in ps]
        xs = [x + _mm3(_split(x), p) for x, p in zip(xs, ps)]
    xs = [_split(x) for x in xs]
    qs = [_mm3(x, o) for x, o in zip(xs, offs)]
    ys = [_mm3(x, _split(r)) for x, r in zip(xs, rhss)]
    q1 = [_split(q) for q in qs]
    q2 = [_mm3(q, q) for q in q1]
    q3 = [_mm3(a, _split(b)) for a, b in zip(q1, q2)]
    rs = [_split(eye - q + b - c) for q, b, c in zip(qs, q2, q3)]
    return [_mm3(r, _split(y)) for r, y in zip(rs, ys)]


def _gdn_kernel(q_ref, k_ref, v_ref, z_ref, sm_ref, smt_ref, nw_ref, o_ref, state_ref, *, sub):
    C = DN_CHUNK
    assert C // DN_BLOCK == 4
    heads = range(DN_HEADS)

    @pl.when(pl.program_id(0) == 0)
    def _():
        state_ref[...] = jnp.zeros_like(state_ref)

    row = lax.broadcasted_iota(jnp.int32, (C, C), 0)
    col = lax.broadcasted_iota(jnp.int32, (C, C), 1)
    incl = row >= col
    strict = row > col
    eye = (row == col).astype(F32)
    shift = DN_BLOCK.bit_length() - 1
    blk = (row >> shift) == (col >> shift)
    tri_b = incl.astype(BF16)
    tri_t_b = (row <= col).astype(BF16)
    lane = lax.broadcasted_iota(jnp.int32, (C, LANES), 1)
    rowi = lax.broadcasted_iota(jnp.int32, (LANES, C), 0)
    g_lane = jnp.logical_and(lane >= SM_G, lane < SM_G + DN_HEADS)
    g_row = jnp.logical_and(rowi >= SM_G, rowi < SM_G + DN_HEADS)
    nw = nw_ref[...]

    prob = [(s, h) for s in range(sub) for h in heads]
    rsl = lambda s: slice(s * C, (s + 1) * C)
    hsl = lambda h: slice(h * DN_HEAD_K, (h + 1) * DN_HEAD_K)
    gsl = lambda h: slice(SM_G + h, SM_G + h + 1)
    sms = [sm_ref[rsl(s), :] for s in range(sub)]
    gcum = [_cumsum_rows(tri_b, jnp.where(g_lane, sm, 0.0)) for sm in sms]
    gcum_t = [_cumsum_cols(jnp.where(g_row, smt_ref[:, rsl(s)], 0.0), tri_t_b) for s in range(sub)]
    eg = [jnp.exp(g) for g in gcum]
    ekd = [jnp.exp(g[C - 1:C, :] - g) for g in gcum]
    eglast = [jnp.exp(g[C - 1:C, :]) for g in gcum]

    kh = [k_ref[rsl(s), hsl(h)] for s, h in prob]
    beta = [sms[s][:, SM_BETA + h:SM_BETA + h + 1] for s, h in prob]
    kb = [k * b for k, b in zip(kh, beta)]
    k_b = [k.astype(BF16) for k in kh]
    kk = [lax.dot_general(a.astype(BF16), b, NT_DIMS, preferred_element_type=F32) for a, b in zip(kb, k_b)]
    qk = [lax.dot_general(q_ref[rsl(s), hsl(h)].astype(BF16), b, NT_DIMS, preferred_element_type=F32)
          for (s, h), b in zip(prob, k_b)]
    dec = []
    for s, h in prob:
        diff = gcum[s][:, gsl(h)] - gcum_t[s][gsl(h), :]
        dec.append(jnp.where(incl, jnp.exp(jnp.where(incl, diff, 0.0)), 0.0))
    ms = [jnp.where(strict, a * d, 0.0) for a, d in zip(kk, dec)]
    aqk = [(a * d).astype(BF16) for a, d in zip(qk, dec)]
    rhs = [jnp.concatenate([v_ref[rsl(s), hsl(h)] * b, a * eg[s][:, gsl(h)]], axis=1)
           for (s, h), b, a in zip(prob, beta, kb)]
    sol = _unit_lower_solve(ms, rhs, eye, blk)
    qd = [(q_ref[rsl(s), hsl(h)] * eg[s][:, gsl(h)]).astype(BF16) for s, h in prob]
    kd = [(k * ekd[s][:, gsl(h)]).astype(BF16) for (s, h), k in zip(prob, kh)]

    st = [state_ref[h] for h in heads]
    for s in range(sub):
        p0 = s * DN_HEADS
        st_b = [a.astype(BF16) for a in st]
        ws = [_dot(sol[p0 + h][:, DN_HEAD_V:].astype(BF16), st_b[h]) for h in heads]
        qs = [_dot(qd[p0 + h], st_b[h]) for h in heads]
        v_b = [(sol[p0 + h][:, :DN_HEAD_V] - ws[h]).astype(BF16) for h in heads]
        o = [qs[h] + _dot(aqk[p0 + h], v_b[h]) for h in heads]
        upd = [lax.dot_general(kd[p0 + h], v_b[h], TN_DIMS, preferred_element_type=F32) for h in heads]
        st = [st[h] * eglast[s][:, gsl(h)] + upd[h] for h in heads]
        for h in heads:
            ms_o = jnp.mean(o[h] * o[h], axis=-1, keepdims=True)
            o_ref[rsl(s), hsl(h)] = (o[h] * lax.rsqrt(ms_o + EPS) * nw * z_ref[rsl(s), hsl(h)]).astype(BF16)
    for h in heads:
        state_ref[h] = st[h]


def _gdn(big, small, small_t, nw, layer, sub=2):
    S = big.shape[0]
    R = sub * DN_CHUNK
    W = DN_HEADS * DN_HEAD_K
    assert R == LANES and W == PJ_TN
    return pl.pallas_call(
        functools.partial(_gdn_kernel, sub=sub),
        out_shape=jax.ShapeDtypeStruct((S, DN_HEADS * DN_HEAD_V), BF16),
        grid=(S // R,),
        in_specs=[
            pl.BlockSpec((R, W), lambda c: (c, PJ_Q)),
            pl.BlockSpec((R, W), lambda c: (c, PJ_K)),
            pl.BlockSpec((R, W), lambda c: (c, PJ_V)),
            pl.BlockSpec((R, W), lambda c: (c, PJ_ZD)),
            pl.BlockSpec((R, LANES), lambda c: (c, 0)),
            pl.BlockSpec((LANES, R), lambda c: (0, c)),
            pl.BlockSpec((None, 1, DN_HEAD_V), lambda c: (layer, 0, 0)),
        ],
        out_specs=pl.BlockSpec((R, W), lambda c: (c, 0)),
        scratch_shapes=[pltpu.VMEM((DN_HEADS, DN_HEAD_K, DN_HEAD_V), F32)],
        compiler_params=_params("arbitrary"), name="gdn",
    )(big, big, big, big, small, small_t, nw)


def _mixout_kernel(h_ref, ys_ref, yd_ref, gs_ref, gd_ref, ws_ref, wd_ref, wo_ref, o_ref):
    merged = gs_ref[...] * _dot(ys_ref[...], ws_ref[...]) + gd_ref[...] * _dot(yd_ref[...], wd_ref[...])
    o_ref[...] = h_ref[...] + _dot(merged.astype(BF16), wo_ref[...])


def _mixout(h, ys, yd, big, ws_b, wd_b, wo_b, layer, tm=512):
    S, D = h.shape
    assert D == PJ_TN
    row = lambda w: pl.BlockSpec((tm, w), lambda i: (i, 0))
    wspec = lambda a: pl.BlockSpec((None,) + a.shape[1:], lambda i: (layer, 0, 0))
    return pl.pallas_call(
        _mixout_kernel,
        out_shape=jax.ShapeDtypeStruct((S, D), F32),
        grid=(S // tm,),
        in_specs=[row(D), row(ys.shape[1]), row(yd.shape[1]),
                  pl.BlockSpec((tm, D), lambda i: (i, PJ_GS)),
                  pl.BlockSpec((tm, D), lambda i: (i, PJ_GD)),
                  wspec(ws_b), wspec(wd_b), wspec(wo_b)],
        out_specs=row(D),
        compiler_params=_params("parallel"), name="mixout",
    )(h, ys, yd, big, big, ws_b, wd_b, wo_b)


def kernel(x, ffn1_norm, ffn1_w_in, ffn1_w_out, mix_norm, w_in, ssm_conv_w, ssm_conv_b, ssm_dt_bias, ssm_a_log, ssm_d, ssm_norm, ssm_w_branch, dn_conv_w, dn_dt_bias, dn_a_log, dn_norm, dn_w_branch, w_out, ffn2_norm, ffn2_w_in, ffn2_w_out, final_norm):
    B, S, D = x.shape
    assert B == 1
    depth = w_in.shape[0]
    H = ssm_dt_bias.shape[1]
    inner = H * SSM_HEAD_DIM
    gn = SSM_GROUPS * SSM_STATE
    dn_k = DN_HEADS * DN_HEAD_K
    dn_v = DN_HEADS * DN_HEAD_V
    assert ssm_norm.shape[1] == inner and ssm_conv_w.shape[2] == inner + 2 * gn
    assert dn_conv_w.shape[2] == 2 * dn_k + dn_v and D == PJ_TN and dn_k == PJ_TN and dn_v == PJ_TN
    assert inner == 2 * PJ_TN and 2 * gn == PJ_TN and SM_LA + H <= LANES

    sizes = (inner, inner + 2 * gn, H, 2 * dn_k + dn_v, dn_v, DN_HEADS, DN_HEADS, D, D)
    offs = [0]
    for sz in sizes:
        offs.append(offs[-1] + sz)
    assert offs[-1] == w_in.shape[2]
    sl = lambda i: slice(offs[i], offs[i + 1])
    z_s, xbc, dt, qkv, z_d, b_d, a_d, g_s, g_d = (w_in[:, :, sl(i)] for i in range(9))

    w_big = jnp.concatenate([xbc, qkv, z_s, z_d, g_s, g_d], axis=-1).astype(BF16)
    pad = jnp.zeros((depth, D, LANES - (SM_LA + H)), F32)
    w_sm = jnp.concatenate([dt, b_d, a_d, dt, pad], axis=-1).astype(BF16)
    w_sm_t = jnp.swapaxes(w_sm, 1, 2)
    zeros_h = jnp.zeros((depth, DN_HEADS), F32)
    tail = jnp.zeros((depth, LANES - (SM_LA + H)), F32)
    bias = jnp.concatenate([ssm_dt_bias, zeros_h, dn_dt_bias, ssm_dt_bias, tail], axis=-1)
    a_log = jnp.concatenate([jnp.zeros_like(ssm_a_log), zeros_h, dn_a_log, ssm_a_log, tail], axis=-1)
    kind = jnp.concatenate([jnp.zeros((H,), F32), jnp.ones((DN_HEADS,), F32),
                            jnp.full((DN_HEADS + H,), 2.0, F32), jnp.ones((LANES - (SM_LA + H),), F32)])
    kind = jnp.broadcast_to(kind, (depth, LANES))
    prow = jnp.stack([bias, a_log, kind] + [jnp.zeros_like(bias)] * (SUBLANES - 3), axis=1)
    pcol = jnp.swapaxes(prow, 1, 2)
    conv_w = jnp.concatenate([ssm_conv_w, dn_conv_w], axis=-1)
    conv_b = jnp.concatenate([ssm_conv_b, jnp.zeros((depth, 2 * dn_k + dn_v), F32)], axis=-1)[:, None, :]

    head_of_col = jnp.arange(inner) // SSM_HEAD_DIM
    lane_id = jnp.arange(2 * LANES) % LANES
    e_dt = (lane_id[:, None] == SM_DT + head_of_col[None, :]).astype(BF16)
    e_la = (lane_id[:, None] == SM_LA + head_of_col[None, :]).astype(BF16)
    dskip = jnp.repeat(ssm_d, SSM_HEAD_DIM, axis=-1)[:, None, :]

    v3 = lambda a: a[:, None, :]
    f1_in, f1_out = ffn1_w_in.astype(BF16), ffn1_w_out.astype(BF16)
    f2_in, f2_out = ffn2_w_in.astype(BF16), ffn2_w_out.astype(BF16)
    ws_b, wd_b, wo_b = ssm_w_branch.astype(BF16), dn_w_branch.astype(BF16), w_out.astype(BF16)
    n1, nm, n2, nf = v3(ffn1_norm), v3(mix_norm), v3(ffn2_norm), final_norm[None, None, :]
    ns, nd = v3(ssm_norm), v3(dn_norm)

    h = x.reshape(S, D)
    for l in range(depth):
        h, u = _ffn(h, n1, f1_in, f1_out, nm, l, "emit_u")
        big = _proj(u, w_big, conv_w, conv_b, l)
        small, small_t = _small_proj(u, w_sm, w_sm_t, prow, pcol, l)
        ys = _ssd(big, small, small_t, e_dt, e_la, dskip, ns, l, H)
        yd = _gdn(big, small, small_t, nd, l)
        h = _mixout(h, ys, yd, big, ws_b, wd_b, wo_b, l)
        last = l == depth - 1
        h = _ffn(h, n2, f2_in, f2_out, nf if last else n2, l, "final" if last else "plain")
    return h.reshape(B, S, D)
```

```python
import functools

import jax
import jax.numpy as jnp
from jax import lax
from jax.experimental import pallas as pl
from jax.experimental.pallas import tpu as pltpu

F32 = jnp.float32
BF16 = jnp.bfloat16
HIGHEST = lax.Precision.HIGHEST

CONV_K = 4
SSM_HEAD_DIM = 64
SSM_GROUPS = 4
SSM_STATE = 128
SSM_CHUNK = 128
DN_HEADS = 8
DN_HEAD_K = 128
DN_HEAD_V = 128
DN_CHUNK = 64
EPS = 1e-6

LANES = 128
SUBLANES = 8
VMEM_LIMIT = 56 * 1024 * 1024

SM_DT = 0
SM_BETA = 32
SM_G = 40
SM_LA = 48

NT_DIMS = (((1,), (1,)), ((), ()))
TN_DIMS = (((0,), (0,)), ((), ()))


def _sigmoid(x):
    return 1.0 / (1.0 + jnp.exp(-x))


def _softplus(x):
    return jnp.maximum(x, 0.0) + jnp.log1p(jnp.exp(-jnp.abs(x)))


def _rms_norm(x, w):
    return x * lax.rsqrt(jnp.mean(x * x, axis=-1, keepdims=True) + EPS) * w


def _dot(a, b, **kw):
    return jnp.dot(a, b, preferred_element_type=F32, **kw)


def _params(*sem):
    return pltpu.CompilerParams(dimension_semantics=sem, vmem_limit_bytes=VMEM_LIMIT)


def _ffn_kernel(h_ref, nw_ref, win_ref, wout_ref, nw2_ref, *rest, d_ff, tf, mode):
    if mode == "emit_u":
        o_ref, u_ref, act_ref = rest
    else:
        o_ref, act_ref = rest
    h = h_ref[...]
    xn = _rms_norm(h, nw_ref[...]).astype(BF16)
    for c in range(d_ff // tf):
        g = _dot(xn, win_ref[:, c * tf:(c + 1) * tf])
        u = _dot(xn, win_ref[:, d_ff + c * tf:d_ff + (c + 1) * tf])
        act_ref[:, c * tf:(c + 1) * tf] = (g * _sigmoid(g) * u).astype(BF16)
    hn = h + 0.5 * _dot(act_ref[...], wout_ref[...])
    if mode == "emit_u":
        o_ref[...] = hn
        u_ref[...] = _rms_norm(hn, nw2_ref[...]).astype(BF16)
    elif mode == "final":
        o_ref[...] = _rms_norm(hn, nw2_ref[...])
    else:
        o_ref[...] = hn


def _ffn(h, nw, win_b, wout_b, nw2, layer, mode, tm=512, tf=256):
    S, D = h.shape
    d_ff = wout_b.shape[1]
    row = pl.BlockSpec((tm, D), lambda i: (i, 0))
    vec = lambda a: pl.BlockSpec((None, 1, D), lambda i: (layer, 0, 0))
    once = pl.Buffered(1)
    in_specs = [
        row,
        vec(nw),
        pl.BlockSpec((None, D, 2 * d_ff), lambda i: (layer, 0, 0), pipeline_mode=once),
        pl.BlockSpec((None, d_ff, D), lambda i: (layer, 0, 0), pipeline_mode=once),
        pl.BlockSpec((None, 1, D), lambda i: (nw2.shape[0] - 1 if mode == "final" else layer, 0, 0)),
    ]
    if mode == "emit_u":
        out_shape = (jax.ShapeDtypeStruct((S, D), F32), jax.ShapeDtypeStruct((S, D), BF16))
        out_specs = (row, row)
    else:
        out_shape = jax.ShapeDtypeStruct((S, D), F32)
        out_specs = row
    return pl.pallas_call(
        functools.partial(_ffn_kernel, d_ff=d_ff, tf=tf, mode=mode),
        out_shape=out_shape, grid=(S // tm,), in_specs=in_specs, out_specs=out_specs,
        scratch_shapes=[pltpu.VMEM((tm, d_ff), BF16)],
        compiler_params=_params("parallel"), name="ffn_" + mode,
    )(h, nw, win_b, wout_b, nw2)


PJ_TN = 1024
PJ_SUB = 256
PJ_ROWS = 256
PJ_CONV_BLOCKS = 6
PJ_Q, PJ_K, PJ_V = 3, 4, 5
PJ_ZS, PJ_ZD, PJ_GS, PJ_GD = 3, 8, 9, 10
PJ_SILU_END = 9
PJ_BLOCKS = 11


def _proj_kernel(u_ref, w_ref, cw_ref, cb_ref, o_ref, ext_ref, *, tm):
    j = pl.program_id(0)
    i = pl.program_id(1)
    is_conv = j < PJ_CONV_BLOCKS
    is_qk = jnp.logical_or(j == PJ_Q, j == PJ_K)
    n_row_units = tm // PJ_ROWS
    units = [(r, c) for c in range(PJ_TN // PJ_SUB) for r in range(n_row_units)]
    rows = lambda r: slice(r * PJ_ROWS, (r + 1) * PJ_ROWS)
    cols = lambda c: slice(c * PJ_SUB, (c + 1) * PJ_SUB)
    lane_cols = lambda c: range(c * PJ_SUB // LANES, (c + 1) * PJ_SUB // LANES)

    @pl.when(jnp.logical_and(is_conv, i == 0))
    def _():
        ext_ref[0, :, 0:SUBLANES, :] = jnp.zeros((PJ_TN // LANES, SUBLANES, LANES), F32)

    def pipelined(epilogue):
        matmul = lambda r, c: _dot(u_ref[rows(r), :], w_ref[:, cols(c)])
        y_next = matmul(*units[0])
        for n, (r, c) in enumerate(units):
            y = y_next
            if n + 1 < len(units):
                y_next = matmul(*units[n + 1])
            for g, lc in enumerate(lane_cols(c)):
                epilogue(r, lc, y[:, g * LANES:(g + 1) * LANES])

    def conv_silu(r, lc, y):
        ls = slice(lc * LANES, (lc + 1) * LANES)
        ext_ref[r, lc, SUBLANES:, :] = y
        acc = cb_ref[:, ls] + cw_ref[CONV_K - 1:CONV_K, ls] * y
        for k in range(CONV_K - 1):
            off = SUBLANES - (CONV_K - 1) + k
            acc = acc + cw_ref[k:k + 1, ls] * ext_ref[r, lc, off:off + PJ_ROWS, :]
        ext_ref[(r + 1) % n_row_units, lc, 0:SUBLANES, :] = y[PJ_ROWS - SUBLANES:, :]
        return acc * _sigmoid(acc)

    @pl.when(jnp.logical_and(is_conv, jnp.logical_not(is_qk)))
    def _():
        def epilogue(r, lc, y):
            o_ref[rows(r), lc * LANES:(lc + 1) * LANES] = conv_silu(r, lc, y)
        pipelined(epilogue)

    @pl.when(is_qk)
    def _():
        scale = jnp.where(j == PJ_Q, DN_HEAD_K ** -0.5, 1.0).astype(F32)

        def epilogue(r, lc, y):
            a = conv_silu(r, lc, y)
            ss = jnp.sum(a * a, axis=-1, keepdims=True)
            o_ref[rows(r), lc * LANES:(lc + 1) * LANES] = a * (lax.rsqrt(ss + EPS) * scale)
        pipelined(epilogue)

    @pl.when(jnp.logical_not(is_conv))
    def _():
        def epilogue(r, lc, y):
            s = _sigmoid(y)
            o_ref[rows(r), lc * LANES:(lc + 1) * LANES] = jnp.where(j < PJ_SILU_END, y * s, s)
        pipelined(epilogue)


def _proj(u, w_big, conv_w, conv_b, layer, tm=512):
    S, D = u.shape
    last_conv = PJ_CONV_BLOCKS - 1
    return pl.pallas_call(
        functools.partial(_proj_kernel, tm=tm),
        out_shape=jax.ShapeDtypeStruct((S, PJ_BLOCKS * PJ_TN), F32),
        grid=(PJ_BLOCKS, S // tm),
        in_specs=[
            pl.BlockSpec((tm, D), lambda j, i: (i, 0)),
            pl.BlockSpec((None, D, PJ_TN), lambda j, i: (layer, 0, j)),
            pl.BlockSpec((None, CONV_K, PJ_TN), lambda j, i: (layer, 0, jnp.minimum(j, last_conv))),
            pl.BlockSpec((None, 1, PJ_TN), lambda j, i: (layer, 0, jnp.minimum(j, last_conv))),
        ],
        out_specs=pl.BlockSpec((tm, PJ_TN), lambda j, i: (i, j)),
        scratch_shapes=[pltpu.VMEM((tm // PJ_ROWS, PJ_TN // LANES, PJ_ROWS + SUBLANES, LANES), F32)],
        compiler_params=_params("arbitrary", "arbitrary"), name="proj",
    )(u, w_big, conv_w, conv_b)


def _small_act(y, bias, a_log, kind):
    sp = _softplus(y + bias)
    return jnp.where(kind == 0.0, sp, jnp.where(kind == 1.0, _sigmoid(y), -jnp.exp(a_log) * sp))


def _small_kernel(u_ref, w_ref, wt_ref, prow_ref, pcol_ref, o_ref, ot_ref):
    u = u_ref[...]
    y = _dot(u, w_ref[...])
    yt = lax.dot_general(wt_ref[...], u, NT_DIMS, preferred_element_type=F32)
    o_ref[...] = _small_act(y, prow_ref[0:1, :], prow_ref[1:2, :], prow_ref[2:3, :])
    ot_ref[...] = _small_act(yt, pcol_ref[:, 0:1], pcol_ref[:, 1:2], pcol_ref[:, 2:3])


def _small_proj(u, w_sm, w_sm_t, prow, pcol, layer, tm=1024):
    S, D = u.shape
    return pl.pallas_call(
        _small_kernel,
        out_shape=(jax.ShapeDtypeStruct((S, LANES), F32), jax.ShapeDtypeStruct((LANES, S), F32)),
        grid=(S // tm,),
        in_specs=[
            pl.BlockSpec((tm, D), lambda i: (i, 0)),
            pl.BlockSpec((None, D, LANES), lambda i: (layer, 0, 0)),
            pl.BlockSpec((None, LANES, D), lambda i: (layer, 0, 0)),
            pl.BlockSpec((None, SUBLANES, LANES), lambda i: (layer, 0, 0)),
            pl.BlockSpec((None, LANES, SUBLANES), lambda i: (layer, 0, 0)),
        ],
        out_specs=(pl.BlockSpec((tm, LANES), lambda i: (i, 0)),
                   pl.BlockSpec((LANES, tm), lambda i: (0, i))),
        compiler_params=_params("parallel"), name="small_proj",
    )(u, w_sm, w_sm_t, prow, pcol)


def _split2(a):
    hi = a.astype(BF16)
    lo = (a - hi.astype(F32)).astype(BF16)
    return jnp.concatenate([hi, lo], axis=1)


def _ssd_kernel(x_ref, b_ref, c_ref, z_ref, sm_ref, smt_ref, edt_ref, ela_ref, dskip_ref,
                nw_ref, o_ref, state_ref, yg_ref, *, n_heads):
    L = SSM_CHUNK
    P = SSM_HEAD_DIM
    N = SSM_STATE
    hg = n_heads // SSM_GROUPS
    gw = hg * P

    @pl.when(pl.program_id(0) == 0)
    def _():
        state_ref[...] = jnp.zeros_like(state_ref)

    row = lax.broadcasted_iota(jnp.int32, (L, L), 0)
    col = lax.broadcasted_iota(jnp.int32, (L, L), 1)
    causal = row >= col
    tri = causal.astype(F32)
    tri_t = (row <= col).astype(F32)
    la_lane = jnp.logical_and(col >= SM_LA, col < SM_LA + n_heads)
    la_row = jnp.logical_and(row >= SM_LA, row < SM_LA + n_heads)

    sm = sm_ref[...]
    acum = _dot(tri, jnp.where(la_lane, sm, 0.0), precision=HIGHEST)
    acum_t = _dot(jnp.where(la_row, smt_ref[...], 0.0), tri_t, precision=HIGHEST)
    eacum = jnp.exp(acum)
    dte = jnp.exp(acum[L - 1:L, :] - acum)
    e_dt = _dot(_split2(sm), edt_ref[...])
    e2 = _dot(_split2(jnp.concatenate([dte, eacum], axis=0)), ela_ref[...])
    e_dte = e2[:L]
    e_ea = e2[L:]

    x = x_ref[...]
    xdt = x * e_dt
    xsc_b = (xdt * e_dte).astype(BF16)
    low_half = col < P
    bm = b_ref[...].astype(BF16)
    cm = c_ref[...].astype(BF16)
    dskip = dskip_ref[...]
    nw = nw_ref[...]
    for g in range(SSM_GROUPS):
        gs = slice(g * gw, (g + 1) * gw)
        bg = bm[:, g * N:(g + 1) * N]
        cg = cm[:, g * N:(g + 1) * N]
        cb = lax.dot_general(cg, bg, NT_DIMS, preferred_element_type=F32)
        st = state_ref[:, gs]
        yoff = _dot(cg, st.astype(BF16)) * e_ea[:, gs]
        state_ref[:, gs] = st * e_ea[L - 1:L, gs] + lax.dot_general(
            bg, xsc_b[:, gs], TN_DIMS, preferred_element_type=F32)
        for pp in range(hg // 2):
            h0 = g * hg + 2 * pp
            ws = []
            for hh in (h0, h0 + 1):
                seg = acum[:, SM_LA + hh:SM_LA + hh + 1] - acum_t[SM_LA + hh:SM_LA + hh + 1, :]
                dec = jnp.where(causal, jnp.exp(jnp.where(causal, seg, 0.0)), 0.0)
                ws.append((cb * dec).astype(BF16))
            lhs = jnp.concatenate(ws, axis=1)
            cs = slice(h0 * P, (h0 + 2) * P)
            xp = xdt[:, cs]
            rhs = jnp.concatenate([jnp.where(low_half, xp, 0.0), jnp.where(low_half, 0.0, xp)],
                                  axis=0).astype(BF16)
            y = _dot(lhs, rhs) + yoff[:, pp * 2 * P:(pp + 1) * 2 * P] + x[:, cs] * dskip[:, cs]
            yg_ref[:, cs] = y * z_ref[:, cs]
        yg = yg_ref[:, gs]
        ms = jnp.sum(yg * yg, axis=-1, keepdims=True) * (1.0 / gw)
        o_ref[:, gs] = (yg * lax.rsqrt(ms + EPS) * nw[:, gs]).astype(BF16)


def _ssd(big, small, small_t, e_dt, e_la, dskip, nw, layer, n_heads):
    S = big.shape[0]
    L = SSM_CHUNK
    inner = n_heads * SSM_HEAD_DIM
    gn = SSM_GROUPS * SSM_STATE
    const = lambda c: (0, 0)
    return pl.pallas_call(
        functools.partial(_ssd_kernel, n_heads=n_heads),
        out_shape=jax.ShapeDtypeStruct((S, inner), BF16),
        grid=(S // L,),
        in_specs=[
            pl.BlockSpec((L, inner), lambda c: (c, 0)),
            pl.BlockSpec((L, gn), lambda c: (c, inner // gn)),
            pl.BlockSpec((L, gn), lambda c: (c, inner // gn + 1)),
            pl.BlockSpec((L, inner), lambda c: (c, PJ_ZS)),
            pl.BlockSpec((L, LANES), lambda c: (c, 0)),
            pl.BlockSpec((LANES, L), lambda c: (0, c)),
            pl.BlockSpec((2 * LANES, inner), const),
            pl.BlockSpec((2 * LANES, inner), const),
            pl.BlockSpec((None, 1, inner), lambda c: (layer, 0, 0)),
            pl.BlockSpec((None, 1, inner), lambda c: (layer, 0, 0)),
        ],
        out_specs=pl.BlockSpec((L, inner), lambda c: (c, 0)),
        scratch_shapes=[pltpu.VMEM((SSM_STATE, inner), F32), pltpu.VMEM((L, inner), F32)],
        compiler_params=_params("arbitrary"), name="ssd",
    )(big, big, big, big, small, small_t, e_dt, e_la, dskip, nw)


DN_BLOCK = 16


def _split(a):
    hi = a.astype(BF16)
    return hi, (a - hi.astype(F32)).astype(BF16)


def _split3(g):
    g1 = g.astype(BF16)
    r1 = g - g1.astype(F32)
    g2 = r1.astype(BF16)
    return g1, g2, (r1 - g2.astype(F32)).astype(BF16)


def _cumsum_rows(tri_b, g):
    g1, g2, g3 = _split3(g)
    return _dot(tri_b, g1) + (_dot(tri_b, g2) + _dot(tri_b, g3))


def _cumsum_cols(g, tri_t_b):
    g1, g2, g3 = _split3(g)
    return _dot(g1, tri_t_b) + (_dot(g2, tri_t_b) + _dot(g3, tri_t_b))


def _pair_lhs(a):
    hi, lo = _split(a)
    return jnp.concatenate([hi, lo, hi], axis=1)


def _pair_rhs(b, upper):
    bd = jnp.concatenate([jnp.where(upper, 0.0, b), jnp.where(upper, b, 0.0)], axis=0)
    hi, lo = _split(bd)
    return jnp.concatenate([hi, hi, lo], axis=0)


def _pair_solve_matrices(ms, eye, blk, upper):
    lhs = lambda vals: [_pair_lhs(v) for v in vals]
    rhs = lambda vals: [_pair_rhs(v, upper) for v in vals]
    mul = lambda ls, rs: [_dot(a, b) for a, b in zip(ls, rs)]
    ps = [jnp.where(blk, m, 0.0) for m in ms]
    xs = [eye - n for n in ps]
    for _ in range(1, DN_BLOCK.bit_length() - 1):
        ps = mul(lhs(ps), rhs(ps))
        xs = [x + d for x, d in zip(xs, mul(lhs(xs), rhs(ps)))]
    xl = lhs(xs)
    qs = mul(xl, rhs([jnp.where(blk, 0.0, m) for m in ms]))
    ql = lhs(qs)
    q2 = mul(ql, rhs(qs))
    q3 = mul(ql, rhs(q2))
    rs = [eye - q + b - c for q, b, c in zip(qs, q2, q3)]
    return [t - eye for t in mul(lhs(rs), rhs(xs))]


def _gdn_kernel(q_ref, k_ref, v_ref, z_ref, sm_ref, smt_ref, nw_ref, o_ref, state_ref):
    C = DN_CHUNK
    R = 2 * C
    assert C // DN_BLOCK == 4 and R == LANES and DN_HEAD_K == LANES and DN_HEAD_V == LANES
    heads = range(DN_HEADS)
    subs = range(2)

    @pl.when(pl.program_id(0) == 0)
    def _():
        state_ref[...] = jnp.zeros_like(state_ref)

    ri = lax.broadcasted_iota(jnp.int32, (C, R), 0)
    cj = lax.broadcasted_iota(jnp.int32, (C, R), 1)
    upper = cj >= C
    cj = jnp.bitwise_and(cj, C - 1)
    incl = ri >= cj
    strict = ri > cj
    eye = (ri == cj).astype(F32)
    shift = DN_BLOCK.bit_length() - 1
    blk = (ri >> shift) == (cj >> shift)
    rr = lax.broadcasted_iota(jnp.int32, (R, R), 0)
    cc = lax.broadcasted_iota(jnp.int32, (R, R), 1)
    same = (rr >> (C.bit_length() - 1)) == (cc >> (C.bit_length() - 1))
    tri_b = jnp.logical_and(same, rr >= cc).astype(BF16)
    tri_t_b = jnp.logical_and(same, rr <= cc).astype(BF16)
    g_lane = jnp.logical_and(cc >= SM_G, cc < SM_G + DN_HEADS)
    g_row = jnp.logical_and(rr >= SM_G, rr < SM_G + DN_HEADS)

    rsl = lambda s: slice(s * C, (s + 1) * C)
    hsl = lambda h: slice(h * DN_HEAD_K, (h + 1) * DN_HEAD_K)
    gsl = lambda h: slice(SM_G + h, SM_G + h + 1)
    sm = sm_ref[...]
    gcum = _cumsum_rows(tri_b, jnp.where(g_lane, sm, 0.0))
    gcum_t = _cumsum_cols(jnp.where(g_row, smt_ref[...], 0.0), tri_t_b)
    eg = jnp.exp(gcum)
    glast = [gcum[(s + 1) * C - 1:(s + 1) * C, :] for s in subs]
    ekd = [jnp.exp(glast[s] - gcum[rsl(s), :]) for s in subs]
    eglast = [jnp.exp(g) for g in glast]
    nw = nw_ref[...]
    zero_b = jnp.zeros((C, LANES), BF16)

    kh = [[k_ref[rsl(s), hsl(h)] for s in subs] for h in heads]
    beta = [[sm[rsl(s), SM_BETA + h:SM_BETA + h + 1] for s in subs] for h in heads]
    kb = [[kh[h][s] * beta[h][s] for s in subs] for h in heads]
    k_nt = [jnp.concatenate([jnp.concatenate([kh[h][0].astype(BF16), zero_b], axis=1),
                             jnp.concatenate([zero_b, kh[h][1].astype(BF16)], axis=1)], axis=0)
            for h in heads]
    kk = [lax.dot_general(jnp.concatenate(kb[h], axis=1).astype(BF16), k_nt[h], NT_DIMS,
                          preferred_element_type=F32) for h in heads]
    qk = [lax.dot_general(jnp.concatenate([q_ref[rsl(s), hsl(h)] for s in subs], axis=1).astype(BF16),
                          k_nt[h], NT_DIMS, preferred_element_type=F32) for h in heads]
    dec = []
    for h in heads:
        gc = jnp.where(upper, gcum[rsl(1), gsl(h)], gcum[rsl(0), gsl(h)])
        dec.append(jnp.where(incl, jnp.exp(jnp.where(incl, gc - gcum_t[gsl(h), :], 0.0)), 0.0))
    aqk = [(qk[h] * dec[h]).astype(BF16) for h in heads]
    tm1 = _pair_solve_matrices([jnp.where(strict, kk[h] * dec[h], 0.0) for h in heads], eye, blk, upper)
    tl = [jnp.concatenate(_split(t), axis=1) for t in tm1]
    rhs = [[jnp.concatenate([v_ref[rsl(s), hsl(h)] * beta[h][s], kb[h][s] * eg[rsl(s), gsl(h)]], axis=1)
            for s in subs] for h in heads]
    zero_r = jnp.zeros((C, DN_HEAD_V + DN_HEAD_K), BF16)
    rb = [[r.astype(BF16) for r in rhs[h]] for h in heads]
    pads = [[jnp.concatenate([rb[h][0], zero_r, rb[h][0], zero_r], axis=0),
             jnp.concatenate([zero_r, rb[h][1], zero_r, rb[h][1]], axis=0)] for h in heads]
    sol = [[rhs[h][s] + _dot(tl[h], pads[h][s]) for s in subs] for h in heads]
    qd = [[(q_ref[rsl(s), hsl(h)] * eg[rsl(s), gsl(h)]).astype(BF16) for s in subs] for h in heads]
    kd = [[(kh[h][s] * ekd[s][:, gsl(h)]).astype(BF16) for s in subs] for h in heads]

    st = [state_ref[h] for h in heads]
    for s in subs:
        st_b = [a.astype(BF16) for a in st]
        ws = [_dot(sol[h][s][:, DN_HEAD_V:].astype(BF16), st_b[h]) for h in heads]
        qs = [_dot(qd[h][s], st_b[h]) for h in heads]
        v_b = [(sol[h][s][:, :DN_HEAD_V] - ws[h]).astype(BF16) for h in heads]
        zero_v = jnp.zeros((C, DN_HEAD_V), BF16)
        v_pad = [jnp.concatenate([v, zero_v] if s == 0 else [zero_v, v], axis=0) for v in v_b]
        o = [qs[h] + _dot(aqk[h], v_pad[h]) for h in heads]
        upd = [lax.dot_general(kd[h][s], v_b[h], TN_DIMS, preferred_element_type=F32) for h in heads]
        st = [st[h] * eglast[s][:, gsl(h)] + upd[h] for h in heads]
        for h in heads:
            ms_o = jnp.mean(o[h] * o[h], axis=-1, keepdims=True)
            o_ref[rsl(s), hsl(h)] = (o[h] * lax.rsqrt(ms_o + EPS) * nw * z_ref[rsl(s), hsl(h)]).astype(BF16)
    for h in heads:
        state_ref[h] = st[h]


def _gdn(big, small, small_t, nw, layer):
    S = big.shape[0]
    R = 2 * DN_CHUNK
    W = DN_HEADS * DN_HEAD_K
    assert R == LANES and W == PJ_TN
    return pl.pallas_call(
        _gdn_kernel,
        out_shape=jax.ShapeDtypeStruct((S, DN_HEADS * DN_HEAD_V), BF16),
        grid=(S // R,),
        in_specs=[
            pl.BlockSpec((R, W), lambda c: (c, PJ_Q)),
            pl.BlockSpec((R, W), lambda c: (c, PJ_K)),
            pl.BlockSpec((R, W), lambda c: (c, PJ_V)),
            pl.BlockSpec((R, W), lambda c: (c, PJ_ZD)),
            pl.BlockSpec((R, LANES), lambda c: (c, 0)),
            pl.BlockSpec((LANES, R), lambda c: (0, c)),
            pl.BlockSpec((None, 1, DN_HEAD_V), lambda c: (layer, 0, 0)),
        ],
        out_specs=pl.BlockSpec((R, W), lambda c: (c, 0)),
        scratch_shapes=[pltpu.VMEM((DN_HEADS, DN_HEAD_K, DN_HEAD_V), F32)],
        compiler_params=_params("arbitrary"), name="gdn",
    )(big, big, big, big, small, small_t, nw)


def _mixout_kernel(h_ref, ys_ref, yd_ref, gs_ref, gd_ref, ws_ref, wd_ref, wo_ref, o_ref):
    merged = gs_ref[...] * _dot(ys_ref[...], ws_ref[...]) + gd_ref[...] * _dot(yd_ref[...], wd_ref[...])
    o_ref[...] = h_ref[...] + _dot(merged.astype(BF16), wo_ref[...])


def _mixout(h, ys, yd, big, ws_b, wd_b, wo_b, layer, tm=512):
    S, D = h.shape
    assert D == PJ_TN
    row = lambda w: pl.BlockSpec((tm, w), lambda i: (i, 0))
    wspec = lambda a: pl.BlockSpec((None,) + a.shape[1:], lambda i: (layer, 0, 0))
    return pl.pallas_call(
        _mixout_kernel,
        out_shape=jax.ShapeDtypeStruct((S, D), F32),
        grid=(S // tm,),
        in_specs=[row(D), row(ys.shape[1]), row(yd.shape[1]),
                  pl.BlockSpec((tm, D), lambda i: (i, PJ_GS)),
                  pl.BlockSpec((tm, D), lambda i: (i, PJ_GD)),
                  wspec(ws_b), wspec(wd_b), wspec(wo_b)],
        out_specs=row(D),
        compiler_params=_params("parallel"), name="mixout",
    )(h, ys, yd, big, big, ws_b, wd_b, wo_b)


def kernel(x, ffn1_norm, ffn1_w_in, ffn1_w_out, mix_norm, w_in, ssm_conv_w, ssm_conv_b, ssm_dt_bias, ssm_a_log, ssm_d, ssm_norm, ssm_w_branch, dn_conv_w, dn_dt_bias, dn_a_log, dn_norm, dn_w_branch, w_out, ffn2_norm, ffn2_w_in, ffn2_w_out, final_norm):
    B, S, D = x.shape
    assert B == 1
    depth = w_in.shape[0]
    H = ssm_dt_bias.shape[1]
    inner = H * SSM_HEAD_DIM
    gn = SSM_GROUPS * SSM_STATE
    dn_k = DN_HEADS * DN_HEAD_K
    dn_v = DN_HEADS * DN_HEAD_V
    assert ssm_norm.shape[1] == inner and ssm_conv_w.shape[2] == inner + 2 * gn
    assert dn_conv_w.shape[2] == 2 * dn_k + dn_v and D == PJ_TN and dn_k == PJ_TN and dn_v == PJ_TN
    assert inner == 2 * PJ_TN and 2 * gn == PJ_TN and SM_LA + H <= LANES

    sizes = (inner, inner + 2 * gn, H, 2 * dn_k + dn_v, dn_v, DN_HEADS, DN_HEADS, D, D)
    offs = [0]
    for sz in sizes:
        offs.append(offs[-1] + sz)
    assert offs[-1] == w_in.shape[2]
    sl = lambda i: slice(offs[i], offs[i + 1])
    z_s, xbc, dt, qkv, z_d, b_d, a_d, g_s, g_d = (w_in[:, :, sl(i)] for i in range(9))

    w_big = jnp.concatenate([xbc, qkv, z_s, z_d, g_s, g_d], axis=-1).astype(BF16)
    pad = jnp.zeros((depth, D, LANES - (SM_LA + H)), F32)
    w_sm = jnp.concatenate([dt, b_d, a_d, dt, pad], axis=-1).astype(BF16)
    w_sm_t = jnp.swapaxes(w_sm, 1, 2)
    zeros_h = jnp.zeros((depth, DN_HEADS), F32)
    tail = jnp.zeros((depth, LANES - (SM_LA + H)), F32)
    bias = jnp.concatenate([ssm_dt_bias, zeros_h, dn_dt_bias, ssm_dt_bias, tail], axis=-1)
    a_log = jnp.concatenate([jnp.zeros_like(ssm_a_log), zeros_h, dn_a_log, ssm_a_log, tail], axis=-1)
    kind = jnp.concatenate([jnp.zeros((H,), F32), jnp.ones((DN_HEADS,), F32),
                            jnp.full((DN_HEADS + H,), 2.0, F32), jnp.ones((LANES - (SM_LA + H),), F32)])
    kind = jnp.broadcast_to(kind, (depth, LANES))
    prow = jnp.stack([bias, a_log, kind] + [jnp.zeros_like(bias)] * (SUBLANES - 3), axis=1)
    pcol = jnp.swapaxes(prow, 1, 2)
    conv_w = jnp.concatenate([ssm_conv_w, dn_conv_w], axis=-1)
    conv_b = jnp.concatenate([ssm_conv_b, jnp.zeros((depth, 2 * dn_k + dn_v), F32)], axis=-1)[:, None, :]

    head_of_col = jnp.arange(inner) // SSM_HEAD_DIM
    lane_id = jnp.arange(2 * LANES) % LANES
    e_dt = (lane_id[:, None] == SM_DT + head_of_col[None, :]).astype(BF16)
    e_la = (lane_id[:, None] == SM_LA + head_of_col[None, :]).astype(BF16)
    dskip = jnp.repeat(ssm_d, SSM_HEAD_DIM, axis=-1)[:, None, :]

    v3 = lambda a: a[:, None, :]
    f1_in, f1_out = ffn1_w_in.astype(BF16), ffn1_w_out.astype(BF16)
    f2_in, f2_out = ffn2_w_in.astype(BF16), ffn2_w_out.astype(BF16)
    ws_b, wd_b, wo_b = ssm_w_branch.astype(BF16), dn_w_branch.astype(BF16), w_out.astype(BF16)
    n1, nm, n2, nf = v3(ffn1_norm), v3(mix_norm), v3(ffn2_norm), final_norm[None, None, :]
    ns, nd = v3(ssm_norm), v3(dn_norm)

    h = x.reshape(S, D)
    for l in range(depth):
        h, u = _ffn(h, n1, f1_in, f1_out, nm, l, "emit_u")
        big = _proj(u, w_big, conv_w, conv_b, l)
        small, small_t = _small_proj(u, w_sm, w_sm_t, prow, pcol, l)
        ys = _ssd(big, small, small_t, e_dt, e_la, dskip, ns, l, H)
        yd = _gdn(big, small, small_t, nd, l)
        h = _mixout(h, ys, yd, big, ws_b, wd_b, wo_b, l)
        last = l == depth - 1
        h = _ffn(h, n2, f2_in, f2_out, nf if last else n2, l, "final" if last else "plain")
    return h.reshape(B, S, D)
```

```python
import functools

import jax
import jax.numpy as jnp
from jax import lax
from jax.experimental import pallas as pl
from jax.experimental.pallas import tpu as pltpu

F32 = jnp.float32
BF16 = jnp.bfloat16

CONV_K = 4
SSM_HEAD_DIM = 64
SSM_GROUPS = 4
SSM_STATE = 128
SSM_CHUNK = 128
DN_HEADS = 8
DN_HEAD_K = 128
DN_HEAD_V = 128
DN_CHUNK = 64
EPS = 1e-6
MASKED = -1e30

LANES = 128
SUBLANES = 8
VMEM_LIMIT = 56 * 1024 * 1024

SM_DT = 0
SM_BETA = 32
SM_G = 40
SM_LA = 48

NT_DIMS = (((1,), (1,)), ((), ()))
TN_DIMS = (((0,), (0,)), ((), ()))


def _sigmoid(x):
    return 0.5 * jnp.tanh(0.5 * x) + 0.5


def _silu(x):
    h = 0.5 * x
    return h * jnp.tanh(h) + h


def _softplus(x):
    return jnp.maximum(x, 0.0) + jnp.log1p(jnp.exp(-jnp.abs(x)))


def _rms_norm(x, w):
    return x * lax.rsqrt(jnp.mean(x * x, axis=-1, keepdims=True) + EPS) * w


def _dot(a, b, **kw):
    return jnp.dot(a, b, preferred_element_type=F32, **kw)


def _params(*sem):
    return pltpu.CompilerParams(dimension_semantics=sem, vmem_limit_bytes=VMEM_LIMIT)


def _ffn_kernel(h_ref, nw_ref, win_ref, wout_ref, nw2_ref, *rest, d_ff, tf, mode):
    if mode == "emit_u":
        o_ref, u_ref, act_ref = rest
    else:
        o_ref, act_ref = rest
    h = h_ref[...]
    xn = _rms_norm(h, nw_ref[...]).astype(BF16)
    for c in range(d_ff // tf):
        g = _dot(xn, win_ref[:, c * tf:(c + 1) * tf])
        u = _dot(xn, win_ref[:, d_ff + c * tf:d_ff + (c + 1) * tf])
        act_ref[:, c * tf:(c + 1) * tf] = (_silu(g) * u).astype(BF16)
    hn = h + 0.5 * _dot(act_ref[...], wout_ref[...])
    if mode == "emit_u":
        o_ref[...] = hn
        u_ref[...] = _rms_norm(hn, nw2_ref[...]).astype(BF16)
    elif mode == "final":
        o_ref[...] = _rms_norm(hn, nw2_ref[...])
    else:
        o_ref[...] = hn


def _ffn(h, nw, win_b, wout_b, nw2, layer, mode, tm=512, tf=256):
    S, D = h.shape
    d_ff = wout_b.shape[1]
    row = pl.BlockSpec((tm, D), lambda i: (i, 0))
    vec = lambda a: pl.BlockSpec((None, 1, D), lambda i: (layer, 0, 0))
    once = pl.Buffered(1)
    in_specs = [
        row,
        vec(nw),
        pl.BlockSpec((None, D, 2 * d_ff), lambda i: (layer, 0, 0), pipeline_mode=once),
        pl.BlockSpec((None, d_ff, D), lambda i: (layer, 0, 0), pipeline_mode=once),
        pl.BlockSpec((None, 1, D), lambda i: (nw2.shape[0] - 1 if mode == "final" else layer, 0, 0)),
    ]
    if mode == "emit_u":
        out_shape = (jax.ShapeDtypeStruct((S, D), F32), jax.ShapeDtypeStruct((S, D), BF16))
        out_specs = (row, row)
    else:
        out_shape = jax.ShapeDtypeStruct((S, D), F32)
        out_specs = row
    return pl.pallas_call(
        functools.partial(_ffn_kernel, d_ff=d_ff, tf=tf, mode=mode),
        out_shape=out_shape, grid=(S // tm,), in_specs=in_specs, out_specs=out_specs,
        scratch_shapes=[pltpu.VMEM((tm, d_ff), BF16)],
        compiler_params=_params("parallel"), name="ffn_" + mode,
    )(h, nw, win_b, wout_b, nw2)


PJ_TN = 1024
PJ_SUB = 256
PJ_ROWS = 256
PJ_CONV_BLOCKS = 6
PJ_Q, PJ_K, PJ_V = 3, 4, 5
PJ_ZS, PJ_ZD, PJ_GS, PJ_GD = 3, 8, 9, 10
PJ_SILU_END = 9
PJ_BLOCKS = 11


def _proj_kernel(u_ref, w_ref, cw_ref, cb_ref, o_ref, ext_ref, *, tm):
    j = pl.program_id(0)
    i = pl.program_id(1)
    is_conv = j < PJ_CONV_BLOCKS
    is_qk = jnp.logical_or(j == PJ_Q, j == PJ_K)
    n_row_units = tm // PJ_ROWS
    units = [(r, c) for c in range(PJ_TN // PJ_SUB) for r in range(n_row_units)]
    rows = lambda r: slice(r * PJ_ROWS, (r + 1) * PJ_ROWS)
    cols = lambda c: slice(c * PJ_SUB, (c + 1) * PJ_SUB)
    lane_cols = lambda c: range(c * PJ_SUB // LANES, (c + 1) * PJ_SUB // LANES)

    @pl.when(jnp.logical_and(is_conv, i == 0))
    def _():
        ext_ref[0, :, 0:SUBLANES, :] = jnp.zeros((PJ_TN // LANES, SUBLANES, LANES), F32)

    def pipelined(epilogue):
        matmul = lambda r, c: _dot(u_ref[rows(r), :], w_ref[:, cols(c)])
        y_next = matmul(*units[0])
        for n, (r, c) in enumerate(units):
            y = y_next
            if n + 1 < len(units):
                y_next = matmul(*units[n + 1])
            for g, lc in enumerate(lane_cols(c)):
                epilogue(r, lc, y[:, g * LANES:(g + 1) * LANES])

    def conv_silu(r, lc, y):
        ls = slice(lc * LANES, (lc + 1) * LANES)
        ext_ref[r, lc, SUBLANES:, :] = y
        acc = cb_ref[:, ls] + cw_ref[CONV_K - 1:CONV_K, ls] * y
        for k in range(CONV_K - 1):
            off = SUBLANES - (CONV_K - 1) + k
            acc = acc + cw_ref[k:k + 1, ls] * ext_ref[r, lc, off:off + PJ_ROWS, :]
        ext_ref[(r + 1) % n_row_units, lc, 0:SUBLANES, :] = y[PJ_ROWS - SUBLANES:, :]
        return _silu(acc)

    @pl.when(jnp.logical_and(is_conv, jnp.logical_not(is_qk)))
    def _():
        def epilogue(r, lc, y):
            o_ref[rows(r), lc * LANES:(lc + 1) * LANES] = conv_silu(r, lc, y)
        pipelined(epilogue)

    @pl.when(is_qk)
    def _():
        scale = jnp.where(j == PJ_Q, DN_HEAD_K ** -0.5, 1.0).astype(F32)

        def epilogue(r, lc, y):
            a = conv_silu(r, lc, y)
            ss = jnp.sum(a * a, axis=-1, keepdims=True)
            o_ref[rows(r), lc * LANES:(lc + 1) * LANES] = a * (lax.rsqrt(ss + EPS) * scale)
        pipelined(epilogue)

    @pl.when(jnp.logical_not(is_conv))
    def _():
        def epilogue(r, lc, y):
            h = 0.5 * y
            t = jnp.tanh(h)
            is_silu = j < PJ_SILU_END
            o_ref[rows(r), lc * LANES:(lc + 1) * LANES] = jnp.where(is_silu, h, 0.5) * t + jnp.where(is_silu, h, 0.5)
        pipelined(epilogue)


def _proj(u, w_big, conv_w, conv_b, layer, tm=1024):
    S, D = u.shape
    last_conv = PJ_CONV_BLOCKS - 1
    return pl.pallas_call(
        functools.partial(_proj_kernel, tm=tm),
        out_shape=jax.ShapeDtypeStruct((S, PJ_BLOCKS * PJ_TN), F32),
        grid=(PJ_BLOCKS, S // tm),
        in_specs=[
            pl.BlockSpec((tm, D), lambda j, i: (i, 0)),
            pl.BlockSpec((None, D, PJ_TN), lambda j, i: (layer, 0, j)),
            pl.BlockSpec((None, CONV_K, PJ_TN), lambda j, i: (layer, 0, jnp.minimum(j, last_conv))),
            pl.BlockSpec((None, 1, PJ_TN), lambda j, i: (layer, 0, jnp.minimum(j, last_conv))),
        ],
        out_specs=pl.BlockSpec((tm, PJ_TN), lambda j, i: (i, j)),
        scratch_shapes=[pltpu.VMEM((tm // PJ_ROWS, PJ_TN // LANES, PJ_ROWS + SUBLANES, LANES), F32)],
        compiler_params=_params("arbitrary", "arbitrary"), name="proj",
    )(u, w_big, conv_w, conv_b)


def _small_act(y, bias, a_log, kind):
    sp = _softplus(y + bias)
    return jnp.where(kind == 0.0, sp, jnp.where(kind == 1.0, _sigmoid(y), -jnp.exp(a_log) * sp))


def _small_kernel(u_ref, w_ref, wt_ref, prow_ref, pcol_ref, o_ref, ot_ref):
    u = u_ref[...]
    y = _dot(u, w_ref[...])
    yt = lax.dot_general(wt_ref[...], u, NT_DIMS, preferred_element_type=F32)
    o_ref[...] = _small_act(y, prow_ref[0:1, :], prow_ref[1:2, :], prow_ref[2:3, :])
    ot_ref[...] = _small_act(yt, pcol_ref[:, 0:1], pcol_ref[:, 1:2], pcol_ref[:, 2:3])


def _small_proj(u, w_sm, w_sm_t, prow, pcol, layer, tm=1024):
    S, D = u.shape
    return pl.pallas_call(
        _small_kernel,
        out_shape=(jax.ShapeDtypeStruct((S, LANES), F32), jax.ShapeDtypeStruct((LANES, S), F32)),
        grid=(S // tm,),
        in_specs=[
            pl.BlockSpec((tm, D), lambda i: (i, 0)),
            pl.BlockSpec((None, D, LANES), lambda i: (layer, 0, 0)),
            pl.BlockSpec((None, LANES, D), lambda i: (layer, 0, 0)),
            pl.BlockSpec((None, SUBLANES, LANES), lambda i: (layer, 0, 0)),
            pl.BlockSpec((None, LANES, SUBLANES), lambda i: (layer, 0, 0)),
        ],
        out_specs=(pl.BlockSpec((tm, LANES), lambda i: (i, 0)),
                   pl.BlockSpec((LANES, tm), lambda i: (0, i))),
        compiler_params=_params("parallel"), name="small_proj",
    )(u, w_sm, w_sm_t, prow, pcol)


def _split2(a):
    hi = a.astype(BF16)
    lo = (a - hi.astype(F32)).astype(BF16)
    return jnp.concatenate([hi, lo], axis=1)


def _ssd_kernel(x_ref, b_ref, c_ref, z_ref, sm_ref, smt_ref, edt_ref, ela_ref, dskip_ref,
                nw_ref, o_ref, state_ref, yg_ref, *, n_heads):
    L = SSM_CHUNK
    P = SSM_HEAD_DIM
    N = SSM_STATE
    hg = n_heads // SSM_GROUPS
    gw = hg * P

    @pl.when(pl.program_id(0) == 0)
    def _():
        state_ref[...] = jnp.zeros_like(state_ref)

    row = lax.broadcasted_iota(jnp.int32, (L, L), 0)
    col = lax.broadcasted_iota(jnp.int32, (L, L), 1)
    causal = row >= col
    la_lane = jnp.logical_and(col >= SM_LA, col < SM_LA + n_heads)
    la_row = jnp.logical_and(row >= SM_LA, row < SM_LA + n_heads)

    sm = sm_ref[...]
    acum = _cumsum_rows(causal.astype(BF16), jnp.where(la_lane, sm, 0.0))
    acum_t = _cumsum_cols(jnp.where(la_row, smt_ref[...], 0.0), (row <= col).astype(BF16))
    eacum = jnp.exp(acum)
    dte = jnp.exp(acum[L - 1:L, :] - acum)
    e_dt = _dot(_split2(sm), edt_ref[...])
    e2 = _dot(_split2(jnp.concatenate([dte, eacum], axis=0)), ela_ref[...])
    e_dte = e2[:L]
    e_ea = e2[L:]

    x = x_ref[...]
    xdt = x * e_dt
    xsc_b = (xdt * e_dte).astype(BF16)
    low_half = col < P
    bm = b_ref[...].astype(BF16)
    cm = c_ref[...].astype(BF16)
    dskip = dskip_ref[...]
    nw = nw_ref[...]
    for g in range(SSM_GROUPS):
        gs = slice(g * gw, (g + 1) * gw)
        bg = bm[:, g * N:(g + 1) * N]
        cg = cm[:, g * N:(g + 1) * N]
        cb = lax.dot_general(cg, bg, NT_DIMS, preferred_element_type=F32)
        st = state_ref[:, gs]
        yoff = _dot(cg, st.astype(BF16)) * e_ea[:, gs]
        state_ref[:, gs] = st * e_ea[L - 1:L, gs] + lax.dot_general(
            bg, xsc_b[:, gs], TN_DIMS, preferred_element_type=F32)
        for pp in range(hg // 2):
            h0 = g * hg + 2 * pp
            ws = []
            for hh in (h0, h0 + 1):
                seg = acum[:, SM_LA + hh:SM_LA + hh + 1] - acum_t[SM_LA + hh:SM_LA + hh + 1, :]
                ws.append((cb * jnp.exp(jnp.where(causal, seg, MASKED))).astype(BF16))
            lhs = jnp.concatenate(ws, axis=1)
            cs = slice(h0 * P, (h0 + 2) * P)
            xp = xdt[:, cs]
            rhs = jnp.concatenate([jnp.where(low_half, xp, 0.0), jnp.where(low_half, 0.0, xp)],
                                  axis=0).astype(BF16)
            y = _dot(lhs, rhs) + yoff[:, pp * 2 * P:(pp + 1) * 2 * P] + x[:, cs] * dskip[:, cs]
            yg_ref[:, cs] = y * z_ref[:, cs]
        yg = yg_ref[:, gs]
        ms = jnp.sum(yg * yg, axis=-1, keepdims=True) * (1.0 / gw)
        o_ref[:, gs] = (yg * lax.rsqrt(ms + EPS) * nw[:, gs]).astype(BF16)


def _ssd(big, small, small_t, e_dt, e_la, dskip, nw, layer, n_heads):
    S = big.shape[0]
    L = SSM_CHUNK
    inner = n_heads * SSM_HEAD_DIM
    gn = SSM_GROUPS * SSM_STATE
    const = lambda c: (0, 0)
    return pl.pallas_call(
        functools.partial(_ssd_kernel, n_heads=n_heads),
        out_shape=jax.ShapeDtypeStruct((S, inner), BF16),
        grid=(S // L,),
        in_specs=[
            pl.BlockSpec((L, inner), lambda c: (c, 0)),
            pl.BlockSpec((L, gn), lambda c: (c, inner // gn)),
            pl.BlockSpec((L, gn), lambda c: (c, inner // gn + 1)),
            pl.BlockSpec((L, inner), lambda c: (c, PJ_ZS)),
            pl.BlockSpec((L, LANES), lambda c: (c, 0)),
            pl.BlockSpec((LANES, L), lambda c: (0, c)),
            pl.BlockSpec((2 * LANES, inner), const),
            pl.BlockSpec((2 * LANES, inner), const),
            pl.BlockSpec((None, 1, inner), lambda c: (layer, 0, 0)),
            pl.BlockSpec((None, 1, inner), lambda c: (layer, 0, 0)),
        ],
        out_specs=pl.BlockSpec((L, inner), lambda c: (c, 0)),
        scratch_shapes=[pltpu.VMEM((SSM_STATE, inner), F32), pltpu.VMEM((L, inner), F32)],
        compiler_params=_params("arbitrary"), name="ssd",
    )(big, big, big, big, small, small_t, e_dt, e_la, dskip, nw)


DN_BLOCK = 16


def _split(a):
    hi = a.astype(BF16)
    return hi, (a - hi.astype(F32)).astype(BF16)


def _split3(g):
    g1 = g.astype(BF16)
    r1 = g - g1.astype(F32)
    g2 = r1.astype(BF16)
    return g1, g2, (r1 - g2.astype(F32)).astype(BF16)


def _cumsum_rows(tri_b, g):
    g1, g2, g3 = _split3(g)
    return _dot(tri_b, g1) + (_dot(tri_b, g2) + _dot(tri_b, g3))


def _cumsum_cols(g, tri_t_b):
    g1, g2, g3 = _split3(g)
    return _dot(g1, tri_t_b) + (_dot(g2, tri_t_b) + _dot(g3, tri_t_b))


def _pair_lhs(a):
    hi, lo = _split(a)
    return jnp.concatenate([hi, lo, hi], axis=1)


def _pair_rhs(b, upper):
    bd = jnp.concatenate([jnp.where(upper, 0.0, b), jnp.where(upper, b, 0.0)], axis=0)
    hi, lo = _split(bd)
    return jnp.concatenate([hi, hi, lo], axis=0)


def _pair_solve_matrices(ms, eye, blk, upper):
    lhs = lambda vals: [_pair_lhs(v) for v in vals]
    rhs = lambda vals: [_pair_rhs(v, upper) for v in vals]
    mul = lambda ls, rs: [_dot(a, b) for a, b in zip(ls, rs)]
    ps = [jnp.where(blk, m, 0.0) for m in ms]
    xs = [eye - n for n in ps]
    for _ in range(1, DN_BLOCK.bit_length() - 1):
        ps = mul(lhs(ps), rhs(ps))
        xs = [x + d for x, d in zip(xs, mul(lhs(xs), rhs(ps)))]
    xl = lhs(xs)
    qs = mul(xl, rhs([jnp.where(blk, 0.0, m) for m in ms]))
    ql = lhs(qs)
    q2 = mul(ql, rhs(qs))
    q3 = mul(ql, rhs(q2))
    rs = [eye - q + b - c for q, b, c in zip(qs, q2, q3)]
    return [t - eye for t in mul(lhs(rs), rhs(xs))]


def _gdn_kernel(q_ref, k_ref, v_ref, z_ref, sm_ref, smt_ref, nw_ref, o_ref, state_ref):
    C = DN_CHUNK
    R = 2 * C
    assert C // DN_BLOCK == 4 and R == LANES and DN_HEAD_K == LANES and DN_HEAD_V == LANES
    heads = range(DN_HEADS)
    subs = range(2)

    @pl.when(pl.program_id(0) == 0)
    def _():
        state_ref[...] = jnp.zeros_like(state_ref)

    ri = lax.broadcasted_iota(jnp.int32, (C, R), 0)
    cj = lax.broadcasted_iota(jnp.int32, (C, R), 1)
    upper = cj >= C
    cj = jnp.bitwise_and(cj, C - 1)
    incl = ri >= cj
    strict = ri > cj
    eye = (ri == cj).astype(F32)
    shift = DN_BLOCK.bit_length() - 1
    blk = (ri >> shift) == (cj >> shift)
    rr = lax.broadcasted_iota(jnp.int32, (R, R), 0)
    cc = lax.broadcasted_iota(jnp.int32, (R, R), 1)
    same = (rr >> (C.bit_length() - 1)) == (cc >> (C.bit_length() - 1))
    tri_b = jnp.logical_and(same, rr >= cc).astype(BF16)
    tri_t_b = jnp.logical_and(same, rr <= cc).astype(BF16)
    g_lane = jnp.logical_and(cc >= SM_G, cc < SM_G + DN_HEADS)
    g_row = jnp.logical_and(rr >= SM_G, rr < SM_G + DN_HEADS)

    rsl = lambda s: slice(s * C, (s + 1) * C)
    hsl = lambda h: slice(h * DN_HEAD_K, (h + 1) * DN_HEAD_K)
    gsl = lambda h: slice(SM_G + h, SM_G + h + 1)
    sm = sm_ref[...]
    gcum = _cumsum_rows(tri_b, jnp.where(g_lane, sm, 0.0))
    gcum_t = _cumsum_cols(jnp.where(g_row, smt_ref[...], 0.0), tri_t_b)
    eg = jnp.exp(gcum)
    glast = [gcum[(s + 1) * C - 1:(s + 1) * C, :] for s in subs]
    ekd = [jnp.exp(glast[s] - gcum[rsl(s), :]) for s in subs]
    eglast = [jnp.exp(g) for g in glast]
    nw = nw_ref[...]
    zero_b = jnp.zeros((C, LANES), BF16)

    kh = [[k_ref[rsl(s), hsl(h)] for s in subs] for h in heads]
    beta = [[sm[rsl(s), SM_BETA + h:SM_BETA + h + 1] for s in subs] for h in heads]
    kb = [[kh[h][s] * beta[h][s] for s in subs] for h in heads]
    k_nt = [jnp.concatenate([jnp.concatenate([kh[h][0].astype(BF16), zero_b], axis=1),
                             jnp.concatenate([zero_b, kh[h][1].astype(BF16)], axis=1)], axis=0)
            for h in heads]
    kq = [lax.dot_general(
        jnp.concatenate([jnp.concatenate(kb[h], axis=1),
                         jnp.concatenate([q_ref[rsl(s), hsl(h)] for s in subs], axis=1)], axis=0).astype(BF16),
        k_nt[h], NT_DIMS, preferred_element_type=F32) for h in heads]
    kk = [a[:C] for a in kq]
    qk = [a[C:] for a in kq]
    dec = []
    for h in heads:
        gc = jnp.where(upper, gcum[rsl(1), gsl(h)], gcum[rsl(0), gsl(h)])
        dec.append(jnp.exp(jnp.where(incl, gc - gcum_t[gsl(h), :], MASKED)))
    aqk = [(qk[h] * dec[h]).astype(BF16) for h in heads]
    tm1 = _pair_solve_matrices([jnp.where(strict, kk[h] * dec[h], 0.0) for h in heads], eye, blk, upper)
    tl = [jnp.concatenate(_split(t), axis=1) for t in tm1]
    rhs = [[jnp.concatenate([v_ref[rsl(s), hsl(h)] * beta[h][s], kb[h][s] * eg[rsl(s), gsl(h)]], axis=1)
            for s in subs] for h in heads]
    zero_r = jnp.zeros((C, DN_HEAD_V + DN_HEAD_K), BF16)
    rb = [[r.astype(BF16) for r in rhs[h]] for h in heads]
    pads = [[jnp.concatenate([rb[h][0], zero_r, rb[h][0], zero_r], axis=0),
             jnp.concatenate([zero_r, rb[h][1], zero_r, rb[h][1]], axis=0)] for h in heads]
    sol = [[rhs[h][s] + _dot(tl[h], pads[h][s]) for s in subs] for h in heads]
    qd = [[(q_ref[rsl(s), hsl(h)] * eg[rsl(s), gsl(h)]).astype(BF16) for s in subs] for h in heads]
    kd = [[(kh[h][s] * ekd[s][:, gsl(h)]).astype(BF16) for s in subs] for h in heads]

    st = [state_ref[h] for h in heads]
    for s in subs:
        st_b = [a.astype(BF16) for a in st]
        wq = [_dot(jnp.concatenate([sol[h][s][:, DN_HEAD_V:].astype(BF16), qd[h][s]], axis=0), st_b[h])
              for h in heads]
        qs = [a[C:] for a in wq]
        v_b = [(sol[h][s][:, :DN_HEAD_V] - wq[h][:C]).astype(BF16) for h in heads]
        zero_v = jnp.zeros((C, DN_HEAD_V), BF16)
        v_pad = [jnp.concatenate([v, zero_v] if s == 0 else [zero_v, v], axis=0) for v in v_b]
        o = [qs[h] + _dot(aqk[h], v_pad[h]) for h in heads]
        upd = [lax.dot_general(kd[h][s], v_b[h], TN_DIMS, preferred_element_type=F32) for h in heads]
        st = [st[h] * eglast[s][:, gsl(h)] + upd[h] for h in heads]
        for h in heads:
            ms_o = jnp.mean(o[h] * o[h], axis=-1, keepdims=True)
            o_ref[rsl(s), hsl(h)] = (o[h] * lax.rsqrt(ms_o + EPS) * nw * z_ref[rsl(s), hsl(h)]).astype(BF16)
    for h in heads:
        state_ref[h] = st[h]


def _gdn(big, small, small_t, nw, layer):
    S = big.shape[0]
    R = 2 * DN_CHUNK
    W = DN_HEADS * DN_HEAD_K
    assert R == LANES and W == PJ_TN
    return pl.pallas_call(
        _gdn_kernel,
        out_shape=jax.ShapeDtypeStruct((S, DN_HEADS * DN_HEAD_V), BF16),
        grid=(S // R,),
        in_specs=[
            pl.BlockSpec((R, W), lambda c: (c, PJ_Q)),
            pl.BlockSpec((R, W), lambda c: (c, PJ_K)),
            pl.BlockSpec((R, W), lambda c: (c, PJ_V)),
            pl.BlockSpec((R, W), lambda c: (c, PJ_ZD)),
            pl.BlockSpec((R, LANES), lambda c: (c, 0)),
            pl.BlockSpec((LANES, R), lambda c: (0, c)),
            pl.BlockSpec((None, 1, DN_HEAD_V), lambda c: (layer, 0, 0)),
        ],
        out_specs=pl.BlockSpec((R, W), lambda c: (c, 0)),
        scratch_shapes=[pltpu.VMEM((DN_HEADS, DN_HEAD_K, DN_HEAD_V), F32)],
        compiler_params=_params("arbitrary"), name="gdn",
    )(big, big, big, big, small, small_t, nw)


def _mixout_kernel(h_ref, ys_ref, yd_ref, gs_ref, gd_ref, ws_ref, wd_ref, wo_ref, o_ref):
    merged = gs_ref[...] * _dot(ys_ref[...], ws_ref[...]) + gd_ref[...] * _dot(yd_ref[...], wd_ref[...])
    o_ref[...] = h_ref[...] + _dot(merged.astype(BF16), wo_ref[...])


def _mixout(h, ys, yd, big, ws_b, wd_b, wo_b, layer, tm=512):
    S, D = h.shape
    assert D == PJ_TN
    row = lambda w: pl.BlockSpec((tm, w), lambda i: (i, 0))
    wspec = lambda a: pl.BlockSpec((None,) + a.shape[1:], lambda i: (layer, 0, 0))
    return pl.pallas_call(
        _mixout_kernel,
        out_shape=jax.ShapeDtypeStruct((S, D), F32),
        grid=(S // tm,),
        in_specs=[row(D), row(ys.shape[1]), row(yd.shape[1]),
                  pl.BlockSpec((tm, D), lambda i: (i, PJ_GS)),
                  pl.BlockSpec((tm, D), lambda i: (i, PJ_GD)),
                  wspec(ws_b), wspec(wd_b), wspec(wo_b)],
        out_specs=row(D),
        compiler_params=_params("parallel"), name="mixout",
    )(h, ys, yd, big, big, ws_b, wd_b, wo_b)


def kernel(x, ffn1_norm, ffn1_w_in, ffn1_w_out, mix_norm, w_in, ssm_conv_w, ssm_conv_b, ssm_dt_bias, ssm_a_log, ssm_d, ssm_norm, ssm_w_branch, dn_conv_w, dn_dt_bias, dn_a_log, dn_norm, dn_w_branch, w_out, ffn2_norm, ffn2_w_in, ffn2_w_out, final_norm):
    B, S, D = x.shape
    assert B == 1
    depth = w_in.shape[0]
    H = ssm_dt_bias.shape[1]
    inner = H * SSM_HEAD_DIM
    gn = SSM_GROUPS * SSM_STATE
    dn_k = DN_HEADS * DN_HEAD_K
    dn_v = DN_HEADS * DN_HEAD_V
    assert ssm_norm.shape[1] == inner and ssm_conv_w.shape[2] == inner + 2 * gn
    assert dn_conv_w.shape[2] == 2 * dn_k + dn_v and D == PJ_TN and dn_k == PJ_TN and dn_v == PJ_TN
    assert inner == 2 * PJ_TN and 2 * gn == PJ_TN and SM_LA + H <= LANES

    sizes = (inner, inner + 2 * gn, H, 2 * dn_k + dn_v, dn_v, DN_HEADS, DN_HEADS, D, D)
    offs = [0]
    for sz in sizes:
        offs.append(offs[-1] + sz)
    assert offs[-1] == w_in.shape[2]
    sl = lambda i: slice(offs[i], offs[i + 1])
    z_s, xbc, dt, qkv, z_d, b_d, a_d, g_s, g_d = (w_in[:, :, sl(i)] for i in range(9))

    w_big = jnp.concatenate([p.astype(BF16) for p in (xbc, qkv, z_s, z_d, g_s, g_d)], axis=-1)
    pad = jnp.zeros((depth, D, LANES - (SM_LA + H)), F32)
    w_sm = jnp.concatenate([dt, b_d, a_d, dt, pad], axis=-1).astype(BF16)
    w_sm_t = jnp.swapaxes(w_sm, 1, 2)
    zeros_h = jnp.zeros((depth, DN_HEADS), F32)
    tail = jnp.zeros((depth, LANES - (SM_LA + H)), F32)
    bias = jnp.concatenate([ssm_dt_bias, zeros_h, dn_dt_bias, ssm_dt_bias, tail], axis=-1)
    a_log = jnp.concatenate([jnp.zeros_like(ssm_a_log), zeros_h, dn_a_log, ssm_a_log, tail], axis=-1)
    kind = jnp.concatenate([jnp.zeros((H,), F32), jnp.ones((DN_HEADS,), F32),
                            jnp.full((DN_HEADS + H,), 2.0, F32), jnp.ones((LANES - (SM_LA + H),), F32)])
    kind = jnp.broadcast_to(kind, (depth, LANES))
    prow = jnp.stack([bias, a_log, kind] + [jnp.zeros_like(bias)] * (SUBLANES - 3), axis=1)
    pcol = jnp.swapaxes(prow, 1, 2)
    conv_w = jnp.concatenate([ssm_conv_w, dn_conv_w], axis=-1)
    conv_b = jnp.concatenate([ssm_conv_b, jnp.zeros((depth, 2 * dn_k + dn_v), F32)], axis=-1)[:, None, :]

    head_of_col = jnp.arange(inner) // SSM_HEAD_DIM
    lane_id = jnp.arange(2 * LANES) % LANES
    e_dt = (lane_id[:, None] == SM_DT + head_of_col[None, :]).astype(BF16)
    e_la = (lane_id[:, None] == SM_LA + head_of_col[None, :]).astype(BF16)
    dskip = jnp.repeat(ssm_d, SSM_HEAD_DIM, axis=-1)[:, None, :]

    v3 = lambda a: a[:, None, :]
    f1_in, f1_out = ffn1_w_in.astype(BF16), ffn1_w_out.astype(BF16)
    f2_in, f2_out = ffn2_w_in.astype(BF16), ffn2_w_out.astype(BF16)
    ws_b, wd_b, wo_b = ssm_w_branch.astype(BF16), dn_w_branch.astype(BF16), w_out.astype(BF16)
    n1, nm, n2, nf = v3(ffn1_norm), v3(mix_norm), v3(ffn2_norm), final_norm[None, None, :]
    ns, nd = v3(ssm_norm), v3(dn_norm)

    h = x.reshape(S, D)
    for l in range(depth):
        h, u = _ffn(h, n1, f1_in, f1_out, nm, l, "emit_u")
        big = _proj(u, w_big, conv_w, conv_b, l)
        small, small_t = _small_proj(u, w_sm, w_sm_t, prow, pcol, l)
        ys = _ssd(big, small, small_t, e_dt, e_la, dskip, ns, l, H)
        yd = _gdn(big, small, small_t, nd, l)
        h = _mixout(h, ys, yd, big, ws_b, wd_b, wo_b, l)
        last = l == depth - 1
        h = _ffn(h, n2, f2_in, f2_out, nf if last else n2, l, "final" if last else "plain")
    return h.reshape(B, S, D)
```

```python
import functools

import jax
import jax.numpy as jnp
from jax import lax
from jax.experimental import pallas as pl
from jax.experimental.pallas import tpu as pltpu

F32 = jnp.float32
BF16 = jnp.bfloat16

CONV_K = 4
SSM_HEAD_DIM = 64
SSM_GROUPS = 4
SSM_STATE = 128
SSM_CHUNK = 128
DN_HEADS = 8
DN_HEAD_K = 128
DN_HEAD_V = 128
DN_CHUNK = 64
EPS = 1e-6
MASKED = -1e30

LANES = 128
SUBLANES = 8
VMEM_LIMIT = 56 * 1024 * 1024

SM_DT = 0
SM_BETA = 32
SM_G = 40
SM_LA = 48

NT_DIMS = (((1,), (1,)), ((), ()))
TN_DIMS = (((0,), (0,)), ((), ()))


def _sigmoid(x):
    return 0.5 * jnp.tanh(0.5 * x) + 0.5


def _silu(x):
    h = 0.5 * x
    return h * jnp.tanh(h) + h


def _softplus(x):
    return jnp.maximum(x, 0.0) + jnp.log1p(jnp.exp(-jnp.abs(x)))


def _rms_norm(x, w):
    return x * lax.rsqrt(jnp.mean(x * x, axis=-1, keepdims=True) + EPS) * w


def _dot(a, b, **kw):
    return jnp.dot(a, b, preferred_element_type=F32, **kw)


def _params(*sem):
    return pltpu.CompilerParams(dimension_semantics=sem, vmem_limit_bytes=VMEM_LIMIT)


def _small_act(y, bias, a_log, kind):
    sp = _softplus(y + bias)
    return jnp.where(kind == 0.0, sp, jnp.where(kind == 1.0, _sigmoid(y), -jnp.exp(a_log) * sp))


def _ffn_kernel(h_ref, nw_ref, win_ref, wout_ref, nw2_ref, *rest, d_ff, tf, mode):
    if mode == "emit_u":
        wsm_ref, wsmt_ref, prow_ref, pcol_ref, o_ref, u_ref, sm_ref, smt_ref, act_ref = rest
    else:
        o_ref, act_ref = rest
    h = h_ref[...]
    xn = _rms_norm(h, nw_ref[...]).astype(BF16)
    for c in range(d_ff // tf):
        g = _dot(xn, win_ref[:, c * tf:(c + 1) * tf])
        u = _dot(xn, win_ref[:, d_ff + c * tf:d_ff + (c + 1) * tf])
        act_ref[:, c * tf:(c + 1) * tf] = (_silu(g) * u).astype(BF16)
    hn = h + 0.5 * _dot(act_ref[...], wout_ref[...])
    if mode == "emit_u":
        o_ref[...] = hn
        u = _rms_norm(hn, nw2_ref[...]).astype(BF16)
        u_ref[...] = u
        y = _dot(u, wsm_ref[...])
        yt = lax.dot_general(wsmt_ref[...], u, NT_DIMS, preferred_element_type=F32)
        sm_ref[...] = _small_act(y, prow_ref[0:1, :], prow_ref[1:2, :], prow_ref[2:3, :])
        smt_ref[...] = _small_act(yt, pcol_ref[:, 0:1], pcol_ref[:, 1:2], pcol_ref[:, 2:3])
    elif mode == "final":
        o_ref[...] = _rms_norm(hn, nw2_ref[...])
    else:
        o_ref[...] = hn


def _ffn(h, nw, win_b, wout_b, nw2, layer, mode, small=(), tm=512, tf=256):
    S, D = h.shape
    d_ff = wout_b.shape[1]
    row = pl.BlockSpec((tm, D), lambda i: (i, 0))
    per_layer = lambda *shape: pl.BlockSpec((None,) + shape, lambda i: (layer, 0, 0))
    once = pl.Buffered(1)
    in_specs = [
        row,
        per_layer(1, D),
        pl.BlockSpec((None, D, 2 * d_ff), lambda i: (layer, 0, 0), pipeline_mode=once),
        pl.BlockSpec((None, d_ff, D), lambda i: (layer, 0, 0), pipeline_mode=once),
        pl.BlockSpec((None, 1, D), lambda i: (nw2.shape[0] - 1 if mode == "final" else layer, 0, 0)),
    ]
    if mode == "emit_u":
        in_specs += [per_layer(D, LANES), per_layer(LANES, D), per_layer(SUBLANES, LANES),
                     per_layer(LANES, SUBLANES)]
        out_shape = (jax.ShapeDtypeStruct((S, D), F32), jax.ShapeDtypeStruct((S, D), BF16),
                     jax.ShapeDtypeStruct((S, LANES), F32), jax.ShapeDtypeStruct((LANES, S), F32))
        out_specs = (row, row, pl.BlockSpec((tm, LANES), lambda i: (i, 0)),
                     pl.BlockSpec((LANES, tm), lambda i: (0, i)))
    else:
        assert not small
        out_shape = jax.ShapeDtypeStruct((S, D), F32)
        out_specs = row
    return pl.pallas_call(
        functools.partial(_ffn_kernel, d_ff=d_ff, tf=tf, mode=mode),
        out_shape=out_shape, grid=(S // tm,), in_specs=in_specs, out_specs=out_specs,
        scratch_shapes=[pltpu.VMEM((tm, d_ff), BF16)],
        compiler_params=_params("parallel"), name="ffn_" + mode,
    )(h, nw, win_b, wout_b, nw2, *small)


PJ_TN = 1024
PJ_SUB = 256
PJ_ROWS = 256
PJ_CONV_BLOCKS = 6
PJ_Q, PJ_K, PJ_V = 3, 4, 5
PJ_ZS, PJ_ZD, PJ_GS, PJ_GD = 3, 8, 9, 10
PJ_SILU_END = 9
PJ_BLOCKS = 11


def _proj_kernel(u_ref, w_ref, cw_ref, cb_ref, o_ref, ext_ref, *, tm):
    j = pl.program_id(0)
    i = pl.program_id(1)
    is_conv = j < PJ_CONV_BLOCKS
    is_qk = jnp.logical_or(j == PJ_Q, j == PJ_K)
    n_row_units = tm // PJ_ROWS
    units = [(r, c) for c in range(PJ_TN // PJ_SUB) for r in range(n_row_units)]
    rows = lambda r: slice(r * PJ_ROWS, (r + 1) * PJ_ROWS)
    cols = lambda c: slice(c * PJ_SUB, (c + 1) * PJ_SUB)
    lane_cols = lambda c: range(c * PJ_SUB // LANES, (c + 1) * PJ_SUB // LANES)

    @pl.when(jnp.logical_and(is_conv, i == 0))
    def _():
        ext_ref[0, :, 0:SUBLANES, :] = jnp.zeros((PJ_TN // LANES, SUBLANES, LANES), F32)

    def pipelined(epilogue):
        matmul = lambda r, c: _dot(u_ref[rows(r), :], w_ref[:, cols(c)])
        y_next = matmul(*units[0])
        for n, (r, c) in enumerate(units):
            y = y_next
            if n + 1 < len(units):
                y_next = matmul(*units[n + 1])
            for g, lc in enumerate(lane_cols(c)):
                epilogue(r, lc, y[:, g * LANES:(g + 1) * LANES])

    def conv_silu(r, lc, y):
        ls = slice(lc * LANES, (lc + 1) * LANES)
        ext_ref[r, lc, SUBLANES:, :] = y
        acc = cb_ref[:, ls] + cw_ref[CONV_K - 1:CONV_K, ls] * y
        for k in range(CONV_K - 1):
            off = SUBLANES - (CONV_K - 1) + k
            acc = acc + cw_ref[k:k + 1, ls] * ext_ref[r, lc, off:off + PJ_ROWS, :]
        ext_ref[(r + 1) % n_row_units, lc, 0:SUBLANES, :] = y[PJ_ROWS - SUBLANES:, :]
        return _silu(acc)

    @pl.when(jnp.logical_and(is_conv, jnp.logical_not(is_qk)))
    def _():
        def epilogue(r, lc, y):
            o_ref[rows(r), lc * LANES:(lc + 1) * LANES] = conv_silu(r, lc, y)
        pipelined(epilogue)

    @pl.when(is_qk)
    def _():
        scale = jnp.where(j == PJ_Q, DN_HEAD_K ** -0.5, 1.0).astype(F32)

        def epilogue(r, lc, y):
            a = conv_silu(r, lc, y)
            ss = jnp.sum(a * a, axis=-1, keepdims=True)
            o_ref[rows(r), lc * LANES:(lc + 1) * LANES] = a * (lax.rsqrt(ss + EPS) * scale)
        pipelined(epilogue)

    @pl.when(jnp.logical_not(is_conv))
    def _():
        def epilogue(r, lc, y):
            h = 0.5 * y
            t = jnp.tanh(h)
            is_silu = j < PJ_SILU_END
            o_ref[rows(r), lc * LANES:(lc + 1) * LANES] = jnp.where(is_silu, h, 0.5) * t + jnp.where(is_silu, h, 0.5)
        pipelined(epilogue)


def _proj(u, w_big, conv_w, conv_b, layer, tm=1024):
    S, D = u.shape
    last_conv = PJ_CONV_BLOCKS - 1
    return pl.pallas_call(
        functools.partial(_proj_kernel, tm=tm),
        out_shape=jax.ShapeDtypeStruct((S, PJ_BLOCKS * PJ_TN), F32),
        grid=(PJ_BLOCKS, S // tm),
        in_specs=[
            pl.BlockSpec((tm, D), lambda j, i: (i, 0)),
            pl.BlockSpec((None, D, PJ_TN), lambda j, i: (layer, 0, j)),
            pl.BlockSpec((None, CONV_K, PJ_TN), lambda j, i: (layer, 0, jnp.minimum(j, last_conv))),
            pl.BlockSpec((None, 1, PJ_TN), lambda j, i: (layer, 0, jnp.minimum(j, last_conv))),
        ],
        out_specs=pl.BlockSpec((tm, PJ_TN), lambda j, i: (i, j)),
        scratch_shapes=[pltpu.VMEM((tm // PJ_ROWS, PJ_TN // LANES, PJ_ROWS + SUBLANES, LANES), F32)],
        compiler_params=_params("arbitrary", "arbitrary"), name="proj",
    )(u, w_big, conv_w, conv_b)


def _split2(a):
    hi = a.astype(BF16)
    lo = (a - hi.astype(F32)).astype(BF16)
    return jnp.concatenate([hi, lo], axis=1)


def _ssd_chunk(x_ref, b_ref, c_ref, z_ref, sm_ref, smt_ref, edt_ref, ela_ref, dskip_ref,
                nw_ref, o_ref, state_ref, yg_ref, *, n_heads):
    L = SSM_CHUNK
    P = SSM_HEAD_DIM
    N = SSM_STATE
    hg = n_heads // SSM_GROUPS
    gw = hg * P

    row = lax.broadcasted_iota(jnp.int32, (L, L), 0)
    col = lax.broadcasted_iota(jnp.int32, (L, L), 1)
    causal = row >= col
    la_lane = jnp.logical_and(col >= SM_LA, col < SM_LA + n_heads)
    la_row = jnp.logical_and(row >= SM_LA, row < SM_LA + n_heads)

    sm = sm_ref[...]
    acum = _cumsum_rows(causal.astype(BF16), jnp.where(la_lane, sm, 0.0))
    acum_t = _cumsum_cols(jnp.where(la_row, smt_ref[...], 0.0), (row <= col).astype(BF16))
    eacum = jnp.exp(acum)
    dte = jnp.exp(acum[L - 1:L, :] - acum)
    e_dt = _dot(_split2(sm), edt_ref[...])
    e2 = _dot(_split2(jnp.concatenate([dte, eacum], axis=0)), ela_ref[...])
    e_dte = e2[:L]
    e_ea = e2[L:]

    x = x_ref[...]
    xdt = x * e_dt
    xsc_b = (xdt * e_dte).astype(BF16)
    low_half = col < P
    bm = b_ref[...].astype(BF16)
    cm = c_ref[...].astype(BF16)
    dskip = dskip_ref[...]
    nw = nw_ref[...]
    for g in range(SSM_GROUPS):
        gs = slice(g * gw, (g + 1) * gw)
        bg = bm[:, g * N:(g + 1) * N]
        cg = cm[:, g * N:(g + 1) * N]
        cb = lax.dot_general(cg, bg, NT_DIMS, preferred_element_type=F32)
        st = state_ref[:, gs]
        yoff = _dot(cg, st.astype(BF16)) * e_ea[:, gs]
        state_ref[:, gs] = st * e_ea[L - 1:L, gs] + lax.dot_general(
            bg, xsc_b[:, gs], TN_DIMS, preferred_element_type=F32)
        for pp in range(hg // 2):
            h0 = g * hg + 2 * pp
            ws = []
            for hh in (h0, h0 + 1):
                seg = acum[:, SM_LA + hh:SM_LA + hh + 1] - acum_t[SM_LA + hh:SM_LA + hh + 1, :]
                ws.append((cb * jnp.exp(jnp.where(causal, seg, MASKED))).astype(BF16))
            lhs = jnp.concatenate(ws, axis=1)
            cs = slice(h0 * P, (h0 + 2) * P)
            xp = xdt[:, cs]
            rhs = jnp.concatenate([jnp.where(low_half, xp, 0.0), jnp.where(low_half, 0.0, xp)],
                                  axis=0).astype(BF16)
            y = _dot(lhs, rhs) + yoff[:, pp * 2 * P:(pp + 1) * 2 * P] + x[:, cs] * dskip[:, cs]
            yg_ref[:, cs] = y * z_ref[:, cs]
        yg = yg_ref[:, gs]
        ms = jnp.sum(yg * yg, axis=-1, keepdims=True) * (1.0 / gw)
        o_ref[:, gs] = (yg * lax.rsqrt(ms + EPS) * nw[:, gs]).astype(BF16)


DN_BLOCK = 16


def _split(a):
    hi = a.astype(BF16)
    return hi, (a - hi.astype(F32)).astype(BF16)


def _split3(g):
    g1 = g.astype(BF16)
    r1 = g - g1.astype(F32)
    g2 = r1.astype(BF16)
    return g1, g2, (r1 - g2.astype(F32)).astype(BF16)


def _cumsum_rows(tri_b, g):
    g1, g2, g3 = _split3(g)
    return _dot(tri_b, g1) + (_dot(tri_b, g2) + _dot(tri_b, g3))


def _cumsum_cols(g, tri_t_b):
    g1, g2, g3 = _split3(g)
    return _dot(g1, tri_t_b) + (_dot(g2, tri_t_b) + _dot(g3, tri_t_b))


def _pair_lhs(a):
    hi, lo = _split(a)
    return jnp.concatenate([hi, lo, hi], axis=1)


def _pair_rhs(b, upper):
    bd = jnp.concatenate([jnp.where(upper, 0.0, b), jnp.where(upper, b, 0.0)], axis=0)
    hi, lo = _split(bd)
    return jnp.concatenate([hi, hi, lo], axis=0)


def _pair_solve_matrices(ms, eye, blk, upper):
    lhs = lambda vals: [_pair_lhs(v) for v in vals]
    rhs = lambda vals: [_pair_rhs(v, upper) for v in vals]
    mul = lambda ls, rs: [_dot(a, b) for a, b in zip(ls, rs)]
    ps = [jnp.where(blk, m, 0.0) for m in ms]
    xs = [eye - n for n in ps]
    for _ in range(1, DN_BLOCK.bit_length() - 1):
        ps = mul(lhs(ps), rhs(ps))
        xs = [x + d for x, d in zip(xs, mul(lhs(xs), rhs(ps)))]
    xl = lhs(xs)
    qs = mul(xl, rhs([jnp.where(blk, 0.0, m) for m in ms]))
    ql = lhs(qs)
    q2 = mul(ql, rhs(qs))
    q3 = mul(ql, rhs(q2))
    rs = [eye - q + b - c for q, b, c in zip(qs, q2, q3)]
    return [t - eye for t in mul(lhs(rs), rhs(xs))]


def _gdn_chunk_pair(q_ref, k_ref, v_ref, z_ref, sm_ref, smt_ref, nw_ref, o_ref, state_ref):
    C = DN_CHUNK
    R = 2 * C
    assert C // DN_BLOCK == 4 and R == LANES and DN_HEAD_K == LANES and DN_HEAD_V == LANES
    heads = range(DN_HEADS)
    subs = range(2)

    ri = lax.broadcasted_iota(jnp.int32, (C, R), 0)
    cj = lax.broadcasted_iota(jnp.int32, (C, R), 1)
    upper = cj >= C
    cj = jnp.bitwise_and(cj, C - 1)
    incl = ri >= cj
    strict = ri > cj
    eye = (ri == cj).astype(F32)
    shift = DN_BLOCK.bit_length() - 1
    blk = (ri >> shift) == (cj >> shift)
    rr = lax.broadcasted_iota(jnp.int32, (R, R), 0)
    cc = lax.broadcasted_iota(jnp.int32, (R, R), 1)
    same = (rr >> (C.bit_length() - 1)) == (cc >> (C.bit_length() - 1))
    tri_b = jnp.logical_and(same, rr >= cc).astype(BF16)
    tri_t_b = jnp.logical_and(same, rr <= cc).astype(BF16)
    g_lane = jnp.logical_and(cc >= SM_G, cc < SM_G + DN_HEADS)
    g_row = jnp.logical_and(rr >= SM_G, rr < SM_G + DN_HEADS)

    rsl = lambda s: slice(s * C, (s + 1) * C)
    hsl = lambda h: slice(h * DN_HEAD_K, (h + 1) * DN_HEAD_K)
    gsl = lambda h: slice(SM_G + h, SM_G + h + 1)
    sm = sm_ref[...]
    gcum = _cumsum_rows(tri_b, jnp.where(g_lane, sm, 0.0))
    gcum_t = _cumsum_cols(jnp.where(g_row, smt_ref[...], 0.0), tri_t_b)
    eg = jnp.exp(gcum)
    glast = [gcum[(s + 1) * C - 1:(s + 1) * C, :] for s in subs]
    ekd = [jnp.exp(glast[s] - gcum[rsl(s), :]) for s in subs]
    eglast = [jnp.exp(g) for g in glast]
    nw = nw_ref[...]
    zero_b = jnp.zeros((C, LANES), BF16)

    kh = [[k_ref[rsl(s), hsl(h)] for s in subs] for h in heads]
    beta = [[sm[rsl(s), SM_BETA + h:SM_BETA + h + 1] for s in subs] for h in heads]
    kb = [[kh[h][s] * beta[h][s] for s in subs] for h in heads]
    k_nt = [jnp.concatenate([jnp.concatenate([kh[h][0].astype(BF16), zero_b], axis=1),
                             jnp.concatenate([zero_b, kh[h][1].astype(BF16)], axis=1)], axis=0)
            for h in heads]
    kq = [lax.dot_general(
        jnp.concatenate([jnp.concatenate(kb[h], axis=1),
                         jnp.concatenate([q_ref[rsl(s), hsl(h)] for s in subs], axis=1)], axis=0).astype(BF16),
        k_nt[h], NT_DIMS, preferred_element_type=F32) for h in heads]
    kk = [a[:C] for a in kq]
    qk = [a[C:] for a in kq]
    dec = []
    for h in heads:
        gc = jnp.where(upper, gcum[rsl(1), gsl(h)], gcum[rsl(0), gsl(h)])
        dec.append(jnp.exp(jnp.where(incl, gc - gcum_t[gsl(h), :], MASKED)))
    aqk = [(qk[h] * dec[h]).astype(BF16) for h in heads]
    tm1 = _pair_solve_matrices([jnp.where(strict, kk[h] * dec[h], 0.0) for h in heads], eye, blk, upper)
    tl = [jnp.concatenate(_split(t), axis=1) for t in tm1]
    rhs = [[jnp.concatenate([v_ref[rsl(s), hsl(h)] * beta[h][s], kb[h][s] * eg[rsl(s), gsl(h)]], axis=1)
            for s in subs] for h in heads]
    zero_r = jnp.zeros((C, DN_HEAD_V + DN_HEAD_K), BF16)
    rb = [[r.astype(BF16) for r in rhs[h]] for h in heads]
    pads = [[jnp.concatenate([rb[h][0], zero_r, rb[h][0], zero_r], axis=0),
             jnp.concatenate([zero_r, rb[h][1], zero_r, rb[h][1]], axis=0)] for h in heads]
    sol = [[rhs[h][s] + _dot(tl[h], pads[h][s]) for s in subs] for h in heads]
    qd = [[(q_ref[rsl(s), hsl(h)] * eg[rsl(s), gsl(h)]).astype(BF16) for s in subs] for h in heads]
    kd = [[(kh[h][s] * ekd[s][:, gsl(h)]).astype(BF16) for s in subs] for h in heads]

    st = [state_ref[h] for h in heads]
    for s in subs:
        st_b = [a.astype(BF16) for a in st]
        wq = [_dot(jnp.concatenate([sol[h][s][:, DN_HEAD_V:].astype(BF16), qd[h][s]], axis=0), st_b[h])
              for h in heads]
        qs = [a[C:] for a in wq]
        v_b = [(sol[h][s][:, :DN_HEAD_V] - wq[h][:C]).astype(BF16) for h in heads]
        zero_v = jnp.zeros((C, DN_HEAD_V), BF16)
        v_pad = [jnp.concatenate([v, zero_v] if s == 0 else [zero_v, v], axis=0) for v in v_b]
        o = [qs[h] + _dot(aqk[h], v_pad[h]) for h in heads]
        upd = [lax.dot_general(kd[h][s], v_b[h], TN_DIMS, preferred_element_type=F32) for h in heads]
        st = [st[h] * eglast[s][:, gsl(h)] + upd[h] for h in heads]
        for h in heads:
            ms_o = jnp.mean(o[h] * o[h], axis=-1, keepdims=True)
            o_ref[rsl(s), hsl(h)] = (o[h] * lax.rsqrt(ms_o + EPS) * nw * z_ref[rsl(s), hsl(h)]).astype(BF16)
    for h in heads:
        state_ref[h] = st[h]


def _mixers_kernel(x_ref, b_ref, c_ref, zs_ref, q_ref, k_ref, v_ref, zd_ref, sm_ref, smt_ref,
                   edt_ref, ela_ref, dskip_ref, ns_ref, nd_ref, ys_ref, yd_ref,
                   s_state_ref, yg_ref, d_state_ref, *, n_heads):
    @pl.when(pl.program_id(0) == 0)
    def _():
        s_state_ref[...] = jnp.zeros_like(s_state_ref)
        d_state_ref[...] = jnp.zeros_like(d_state_ref)

    _ssd_chunk(x_ref, b_ref, c_ref, zs_ref, sm_ref, smt_ref, edt_ref, ela_ref, dskip_ref, ns_ref,
                ys_ref, s_state_ref, yg_ref, n_heads=n_heads)
    _gdn_chunk_pair(q_ref, k_ref, v_ref, zd_ref, sm_ref, smt_ref, nd_ref, yd_ref, d_state_ref)


def _mixers(big, small, small_t, e_dt, e_la, dskip, ns, nd, layer, n_heads):
    S = big.shape[0]
    L = SSM_CHUNK
    inner = n_heads * SSM_HEAD_DIM
    gn = SSM_GROUPS * SSM_STATE
    W = DN_HEADS * DN_HEAD_K
    assert L == 2 * DN_CHUNK and L == LANES and W == PJ_TN
    const = lambda c: (0, 0)
    per_layer = lambda w: pl.BlockSpec((None, 1, w), lambda c: (layer, 0, 0))
    return pl.pallas_call(
        functools.partial(_mixers_kernel, n_heads=n_heads),
        out_shape=(jax.ShapeDtypeStruct((S, inner), BF16),
                   jax.ShapeDtypeStruct((S, DN_HEADS * DN_HEAD_V), BF16)),
        grid=(S // L,),
        in_specs=[
            pl.BlockSpec((L, inner), lambda c: (c, 0)),
            pl.BlockSpec((L, gn), lambda c: (c, inner // gn)),
            pl.BlockSpec((L, gn), lambda c: (c, inner // gn + 1)),
            pl.BlockSpec((L, inner), lambda c: (c, PJ_ZS)),
            pl.BlockSpec((L, W), lambda c: (c, PJ_Q)),
            pl.BlockSpec((L, W), lambda c: (c, PJ_K)),
            pl.BlockSpec((L, W), lambda c: (c, PJ_V)),
            pl.BlockSpec((L, W), lambda c: (c, PJ_ZD)),
            pl.BlockSpec((L, LANES), lambda c: (c, 0)),
            pl.BlockSpec((LANES, L), lambda c: (0, c)),
            pl.BlockSpec((2 * LANES, inner), const),
            pl.BlockSpec((2 * LANES, inner), const),
            per_layer(inner), per_layer(inner), per_layer(DN_HEAD_V),
        ],
        out_specs=(pl.BlockSpec((L, inner), lambda c: (c, 0)), pl.BlockSpec((L, W), lambda c: (c, 0))),
        scratch_shapes=[pltpu.VMEM((SSM_STATE, inner), F32), pltpu.VMEM((L, inner), F32),
                        pltpu.VMEM((DN_HEADS, DN_HEAD_K, DN_HEAD_V), F32)],
        compiler_params=_params("arbitrary"), name="mixers",
    )(big, big, big, big, big, big, big, big, small, small_t, e_dt, e_la, dskip, ns, nd)


def _mixout_kernel(h_ref, ys_ref, yd_ref, gs_ref, gd_ref, ws_ref, wd_ref, wo_ref, o_ref):
    merged = gs_ref[...] * _dot(ys_ref[...], ws_ref[...]) + gd_ref[...] * _dot(yd_ref[...], wd_ref[...])
    o_ref[...] = h_ref[...] + _dot(merged.astype(BF16), wo_ref[...])


def _mixout(h, ys, yd, big, ws_b, wd_b, wo_b, layer, tm=512):
    S, D = h.shape
    assert D == PJ_TN
    row = lambda w: pl.BlockSpec((tm, w), lambda i: (i, 0))
    wspec = lambda a: pl.BlockSpec((None,) + a.shape[1:], lambda i: (layer, 0, 0))
    return pl.pallas_call(
        _mixout_kernel,
        out_shape=jax.ShapeDtypeStruct((S, D), F32),
        grid=(S // tm,),
        in_specs=[row(D), row(ys.shape[1]), row(yd.shape[1]),
                  pl.BlockSpec((tm, D), lambda i: (i, PJ_GS)),
                  pl.BlockSpec((tm, D), lambda i: (i, PJ_GD)),
                  wspec(ws_b), wspec(wd_b), wspec(wo_b)],
        out_specs=row(D),
        compiler_params=_params("parallel"), name="mixout",
    )(h, ys, yd, big, big, ws_b, wd_b, wo_b)


def kernel(x, ffn1_norm, ffn1_w_in, ffn1_w_out, mix_norm, w_in, ssm_conv_w, ssm_conv_b, ssm_dt_bias, ssm_a_log, ssm_d, ssm_norm, ssm_w_branch, dn_conv_w, dn_dt_bias, dn_a_log, dn_norm, dn_w_branch, w_out, ffn2_norm, ffn2_w_in, ffn2_w_out, final_norm):
    B, S, D = x.shape
    assert B == 1
    depth = w_in.shape[0]
    H = ssm_dt_bias.shape[1]
    inner = H * SSM_HEAD_DIM
    gn = SSM_GROUPS * SSM_STATE
    dn_k = DN_HEADS * DN_HEAD_K
    dn_v = DN_HEADS * DN_HEAD_V
    assert ssm_norm.shape[1] == inner and ssm_conv_w.shape[2] == inner + 2 * gn
    assert dn_conv_w.shape[2] == 2 * dn_k + dn_v and D == PJ_TN and dn_k == PJ_TN and dn_v == PJ_TN
    assert inner == 2 * PJ_TN and 2 * gn == PJ_TN and SM_LA + H <= LANES

    sizes = (inner, inner + 2 * gn, H, 2 * dn_k + dn_v, dn_v, DN_HEADS, DN_HEADS, D, D)
    offs = [0]
    for sz in sizes:
        offs.append(offs[-1] + sz)
    assert offs[-1] == w_in.shape[2]
    sl = lambda i: slice(offs[i], offs[i + 1])
    z_s, xbc, dt, qkv, z_d, b_d, a_d, g_s, g_d = (w_in[:, :, sl(i)] for i in range(9))

    w_big = jnp.concatenate([p.astype(BF16) for p in (xbc, qkv, z_s, z_d, g_s, g_d)], axis=-1)
    pad = jnp.zeros((depth, D, LANES - (SM_LA + H)), F32)
    w_sm = jnp.concatenate([dt, b_d, a_d, dt, pad], axis=-1).astype(BF16)
    w_sm_t = jnp.swapaxes(w_sm, 1, 2)
    zeros_h = jnp.zeros((depth, DN_HEADS), F32)
    tail = jnp.zeros((depth, LANES - (SM_LA + H)), F32)
    bias = jnp.concatenate([ssm_dt_bias, zeros_h, dn_dt_bias, ssm_dt_bias, tail], axis=-1)
    a_log = jnp.concatenate([jnp.zeros_like(ssm_a_log), zeros_h, dn_a_log, ssm_a_log, tail], axis=-1)
    kind = jnp.concatenate([jnp.zeros((H,), F32), jnp.ones((DN_HEADS,), F32),
                            jnp.full((DN_HEADS + H,), 2.0, F32), jnp.ones((LANES - (SM_LA + H),), F32)])
    kind = jnp.broadcast_to(kind, (depth, LANES))
    prow = jnp.stack([bias, a_log, kind] + [jnp.zeros_like(bias)] * (SUBLANES - 3), axis=1)
    pcol = jnp.swapaxes(prow, 1, 2)
    conv_w = jnp.concatenate([ssm_conv_w, dn_conv_w], axis=-1)
    conv_b = jnp.concatenate([ssm_conv_b, jnp.zeros((depth, 2 * dn_k + dn_v), F32)], axis=-1)[:, None, :]

    head_of_col = jnp.arange(inner) // SSM_HEAD_DIM
    lane_id = jnp.arange(2 * LANES) % LANES
    e_dt = (lane_id[:, None] == SM_DT + head_of_col[None, :]).astype(BF16)
    e_la = (lane_id[:, None] == SM_LA + head_of_col[None, :]).astype(BF16)
    dskip = jnp.repeat(ssm_d, SSM_HEAD_DIM, axis=-1)[:, None, :]

    v3 = lambda a: a[:, None, :]
    f1_in, f1_out = ffn1_w_in.astype(BF16), ffn1_w_out.astype(BF16)
    f2_in, f2_out = ffn2_w_in.astype(BF16), ffn2_w_out.astype(BF16)
    ws_b, wd_b, wo_b = ssm_w_branch.astype(BF16), dn_w_branch.astype(BF16), w_out.astype(BF16)
    n1, nm, n2, nf = v3(ffn1_norm), v3(mix_norm), v3(ffn2_norm), final_norm[None, None, :]
    ns, nd = v3(ssm_norm), v3(dn_norm)

    h = x.reshape(S, D)
    for l in range(depth):
        h, u, small, small_t = _ffn(h, n1, f1_in, f1_out, nm, l, "emit_u", (w_sm, w_sm_t, prow, pcol))
        big = _proj(u, w_big, conv_w, conv_b, l)
        ys, yd = _mixers(big, small, small_t, e_dt, e_la, dskip, ns, nd, l, H)
        h = _mixout(h, ys, yd, big, ws_b, wd_b, wo_b, l)
        last = l == depth - 1
        h = _ffn(h, n2, f2_in, f2_out, nf if last else n2, l, "final" if last else "plain")
    return h.reshape(B, S, D)
```

```python
import functools

import jax
import jax.numpy as jnp
from jax import lax
from jax.experimental import pallas as pl
from jax.experimental.pallas import tpu as pltpu

F32 = jnp.float32
BF16 = jnp.bfloat16

CONV_K = 4
SSM_HEAD_DIM = 64
SSM_GROUPS = 4
SSM_STATE = 128
SSM_CHUNK = 128
DN_HEADS = 8
DN_HEAD_K = 128
DN_HEAD_V = 128
DN_CHUNK = 64
EPS = 1e-6
MASKED = -1e30

LANES = 128
SUBLANES = 8
VMEM_LIMIT = 56 * 1024 * 1024

SM_DT = 0
SM_BETA = 32
SM_G = 40
SM_LA = 48

NT_DIMS = (((1,), (1,)), ((), ()))
TN_DIMS = (((0,), (0,)), ((), ()))


def _sigmoid(x):
    return 0.5 * jnp.tanh(0.5 * x) + 0.5


def _silu(x):
    h = 0.5 * x
    return h * jnp.tanh(h) + h


def _softplus(x):
    return jnp.maximum(x, 0.0) + jnp.log1p(jnp.exp(-jnp.abs(x)))


def _rms_norm(x, w):
    return x * lax.rsqrt(jnp.mean(x * x, axis=-1, keepdims=True) + EPS) * w


def _dot(a, b, **kw):
    return jnp.dot(a, b, preferred_element_type=F32, **kw)


def _params(*sem):
    return pltpu.CompilerParams(dimension_semantics=sem, vmem_limit_bytes=VMEM_LIMIT)


def _small_act(y, bias, a_log, kind):
    sp = _softplus(y + bias)
    return jnp.where(kind == 0.0, sp, jnp.where(kind == 1.0, _sigmoid(y), -jnp.exp(a_log) * sp))


def _ffn_kernel(h_ref, nw_ref, win_ref, wout_ref, nw2_ref, *rest, d_ff, tf, mode):
    if mode == "emit_u":
        wsm_ref, wsmt_ref, prow_ref, pcol_ref, o_ref, u_ref, sm_ref, smt_ref, act_ref = rest
    else:
        o_ref, act_ref = rest
    h = h_ref[...]
    xn = _rms_norm(h, nw_ref[...]).astype(BF16)
    for c in range(d_ff // tf):
        g = _dot(xn, win_ref[:, c * tf:(c + 1) * tf])
        u = _dot(xn, win_ref[:, d_ff + c * tf:d_ff + (c + 1) * tf])
        act_ref[:, c * tf:(c + 1) * tf] = (_silu(g) * u).astype(BF16)
    hn = h + 0.5 * _dot(act_ref[...], wout_ref[...])
    if mode == "emit_u":
        o_ref[...] = hn
        u = _rms_norm(hn, nw2_ref[...]).astype(BF16)
        u_ref[...] = u
        y = _dot(u, wsm_ref[...])
        yt = lax.dot_general(wsmt_ref[...], u, NT_DIMS, preferred_element_type=F32)
        sm_ref[...] = _small_act(y, prow_ref[0:1, :], prow_ref[1:2, :], prow_ref[2:3, :])
        smt_ref[...] = _small_act(yt, pcol_ref[:, 0:1], pcol_ref[:, 1:2], pcol_ref[:, 2:3])
    elif mode == "final":
        o_ref[...] = _rms_norm(hn, nw2_ref[...])
    else:
        o_ref[...] = hn


def _ffn(h, nw, win_b, wout_b, nw2, layer, mode, small=(), tm=512, tf=256):
    S, D = h.shape
    d_ff = wout_b.shape[1]
    row = pl.BlockSpec((tm, D), lambda i: (i, 0))
    per_layer = lambda *shape: pl.BlockSpec((None,) + shape, lambda i: (layer, 0, 0))
    once = pl.Buffered(1)
    in_specs = [
        row,
        per_layer(1, D),
        pl.BlockSpec((None, D, 2 * d_ff), lambda i: (layer, 0, 0), pipeline_mode=once),
        pl.BlockSpec((None, d_ff, D), lambda i: (layer, 0, 0), pipeline_mode=once),
        pl.BlockSpec((None, 1, D), lambda i: (nw2.shape[0] - 1 if mode == "final" else layer, 0, 0)),
    ]
    if mode == "emit_u":
        in_specs += [per_layer(D, LANES), per_layer(LANES, D), per_layer(SUBLANES, LANES),
                     per_layer(LANES, SUBLANES)]
        out_shape = (jax.ShapeDtypeStruct((S, D), F32), jax.ShapeDtypeStruct((S, D), BF16),
                     jax.ShapeDtypeStruct((S, LANES), F32), jax.ShapeDtypeStruct((LANES, S), F32))
        out_specs = (row, row, pl.BlockSpec((tm, LANES), lambda i: (i, 0)),
                     pl.BlockSpec((LANES, tm), lambda i: (0, i)))
    else:
        assert not small
        out_shape = jax.ShapeDtypeStruct((S, D), F32)
        out_specs = row
    return pl.pallas_call(
        functools.partial(_ffn_kernel, d_ff=d_ff, tf=tf, mode=mode),
        out_shape=out_shape, grid=(S // tm,), in_specs=in_specs, out_specs=out_specs,
        scratch_shapes=[pltpu.VMEM((tm, d_ff), BF16)],
        compiler_params=_params("parallel"), name="ffn_" + mode,
    )(h, nw, win_b, wout_b, nw2, *small)


PJ_TN = 1024
PJ_SUB = 256
PJ_ROWS = 256
PJ_CONV_BLOCKS = 6
PJ_Q, PJ_K, PJ_V = 3, 4, 5
PJ_ZS, PJ_ZD, PJ_GS, PJ_GD = 3, 8, 9, 10
PJ_SILU_END = 9
PJ_BLOCKS = 11


def _proj_kernel(u_ref, w_ref, cw_ref, cb_ref, o_ref, ext_ref, *, tm):
    j = pl.program_id(0)
    i = pl.program_id(1)
    is_conv = j < PJ_CONV_BLOCKS
    is_qk = jnp.logical_or(j == PJ_Q, j == PJ_K)
    n_row_units = tm // PJ_ROWS
    units = [(r, c) for c in range(PJ_TN // PJ_SUB) for r in range(n_row_units)]
    rows = lambda r: slice(r * PJ_ROWS, (r + 1) * PJ_ROWS)
    cols = lambda c: slice(c * PJ_SUB, (c + 1) * PJ_SUB)
    lane_cols = lambda c: range(c * PJ_SUB // LANES, (c + 1) * PJ_SUB // LANES)

    @pl.when(jnp.logical_and(is_conv, i == 0))
    def _():
        ext_ref[0, :, 0:SUBLANES, :] = jnp.zeros((PJ_TN // LANES, SUBLANES, LANES), F32)

    def pipelined(epilogue):
        matmul = lambda r, c: _dot(u_ref[rows(r), :], w_ref[:, cols(c)])
        y_next = matmul(*units[0])
        for n, (r, c) in enumerate(units):
            y = y_next
            if n + 1 < len(units):
                y_next = matmul(*units[n + 1])
            for g, lc in enumerate(lane_cols(c)):
                epilogue(r, lc, y[:, g * LANES:(g + 1) * LANES])

    def conv_silu(r, lc, y):
        ls = slice(lc * LANES, (lc + 1) * LANES)
        ext_ref[r, lc, SUBLANES:, :] = y
        h = 0.5 * cb_ref[:, ls] + (0.5 * cw_ref[CONV_K - 1:CONV_K, ls]) * y
        for k in range(CONV_K - 1):
            off = SUBLANES - (CONV_K - 1) + k
            h = h + (0.5 * cw_ref[k:k + 1, ls]) * ext_ref[r, lc, off:off + PJ_ROWS, :]
        ext_ref[(r + 1) % n_row_units, lc, 0:SUBLANES, :] = y[PJ_ROWS - SUBLANES:, :]
        return h * jnp.tanh(h) + h

    @pl.when(jnp.logical_and(is_conv, jnp.logical_not(is_qk)))
    def _():
        def epilogue(r, lc, y):
            o_ref[rows(r), lc * LANES:(lc + 1) * LANES] = conv_silu(r, lc, y)
        pipelined(epilogue)

    @pl.when(is_qk)
    def _():
        scale = jnp.where(j == PJ_Q, DN_HEAD_K ** -0.5, 1.0).astype(F32)

        def epilogue(r, lc, y):
            a = conv_silu(r, lc, y)
            ss = jnp.sum(a * a, axis=-1, keepdims=True)
            o_ref[rows(r), lc * LANES:(lc + 1) * LANES] = a * (lax.rsqrt(ss + EPS) * scale)
        pipelined(epilogue)

    @pl.when(jnp.logical_and(jnp.logical_not(is_conv), j < PJ_SILU_END))
    def _():
        def epilogue(r, lc, y):
            o_ref[rows(r), lc * LANES:(lc + 1) * LANES] = _silu(y)
        pipelined(epilogue)

    @pl.when(j >= PJ_SILU_END)
    def _():
        def epilogue(r, lc, y):
            o_ref[rows(r), lc * LANES:(lc + 1) * LANES] = y
        pipelined(epilogue)


def _proj(u, w_big, conv_w, conv_b, layer, tm=2048):
    S, D = u.shape
    last_conv = PJ_CONV_BLOCKS - 1
    return pl.pallas_call(
        functools.partial(_proj_kernel, tm=tm),
        out_shape=jax.ShapeDtypeStruct((S, PJ_BLOCKS * PJ_TN), F32),
        grid=(PJ_BLOCKS, S // tm),
        in_specs=[
            pl.BlockSpec((tm, D), lambda j, i: (i, 0)),
            pl.BlockSpec((None, D, PJ_TN), lambda j, i: (layer, 0, j)),
            pl.BlockSpec((None, CONV_K, PJ_TN), lambda j, i: (layer, 0, jnp.minimum(j, last_conv))),
            pl.BlockSpec((None, 1, PJ_TN), lambda j, i: (layer, 0, jnp.minimum(j, last_conv))),
        ],
        out_specs=pl.BlockSpec((tm, PJ_TN), lambda j, i: (i, j)),
        scratch_shapes=[pltpu.VMEM((tm // PJ_ROWS, PJ_TN // LANES, PJ_ROWS + SUBLANES, LANES), F32)],
        compiler_params=_params("arbitrary", "arbitrary"), name="proj",
    )(u, w_big, conv_w, conv_b)


def _split2(a):
    hi = a.astype(BF16)
    lo = (a - hi.astype(F32)).astype(BF16)
    return jnp.concatenate([hi, lo], axis=1)


def _ssd_chunk(x_ref, b_ref, c_ref, z_ref, sm_ref, smt_ref, edt_ref, ela_ref, dskip_ref,
                nw_ref, o_ref, state_ref, yg_ref, *, n_heads):
    L = SSM_CHUNK
    P = SSM_HEAD_DIM
    N = SSM_STATE
    hg = n_heads // SSM_GROUPS
    gw = hg * P

    row = lax.broadcasted_iota(jnp.int32, (L, L), 0)
    col = lax.broadcasted_iota(jnp.int32, (L, L), 1)
    causal = row >= col
    la_lane = jnp.logical_and(col >= SM_LA, col < SM_LA + n_heads)
    la_row = jnp.logical_and(row >= SM_LA, row < SM_LA + n_heads)

    sm = sm_ref[...]
    acum = _cumsum_rows(causal.astype(BF16), jnp.where(la_lane, sm, 0.0))
    acum_t = _cumsum_cols(jnp.where(la_row, smt_ref[...], 0.0), (row <= col).astype(BF16))
    eacum = jnp.exp(acum)
    dte = jnp.exp(acum[L - 1:L, :] - acum)
    e_dt = _dot(_split2(sm), edt_ref[...])
    e2 = _dot(_split2(jnp.concatenate([dte, eacum], axis=0)), ela_ref[...])
    e_dte = e2[:L]
    e_ea = e2[L:]

    x = x_ref[...]
    xdt = x * e_dt
    xsc_b = (xdt * e_dte).astype(BF16)
    low_half = col < P
    bm = b_ref[...].astype(BF16)
    cm = c_ref[...].astype(BF16)
    dskip = dskip_ref[...]
    nw = nw_ref[...]
    for g in range(SSM_GROUPS):
        gs = slice(g * gw, (g + 1) * gw)
        bg = bm[:, g * N:(g + 1) * N]
        cg = cm[:, g * N:(g + 1) * N]
        cb = lax.dot_general(cg, bg, NT_DIMS, preferred_element_type=F32)
        st = state_ref[:, gs]
        yoff = _dot(cg, st.astype(BF16)) * e_ea[:, gs]
        state_ref[:, gs] = st * e_ea[L - 1:L, gs] + lax.dot_general(
            bg, xsc_b[:, gs], TN_DIMS, preferred_element_type=F32)
        for pp in range(hg // 2):
            h0 = g * hg + 2 * pp
            ws = []
            for hh in (h0, h0 + 1):
                seg = acum[:, SM_LA + hh:SM_LA + hh + 1] - acum_t[SM_LA + hh:SM_LA + hh + 1, :]
                ws.append((cb * jnp.exp(jnp.where(causal, seg, MASKED))).astype(BF16))
            lhs = jnp.concatenate(ws, axis=1)
            cs = slice(h0 * P, (h0 + 2) * P)
            xp = xdt[:, cs]
            rhs = jnp.concatenate([jnp.where(low_half, xp, 0.0), jnp.where(low_half, 0.0, xp)],
                                  axis=0).astype(BF16)
            y = _dot(lhs, rhs) + yoff[:, pp * 2 * P:(pp + 1) * 2 * P] + x[:, cs] * dskip[:, cs]
            yg_ref[:, cs] = y * z_ref[:, cs]
        yg = yg_ref[:, gs]
        ms = jnp.sum(yg * yg, axis=-1, keepdims=True) * (1.0 / gw)
        o_ref[:, gs] = (yg * lax.rsqrt(ms + EPS) * nw[:, gs]).astype(BF16)


DN_BLOCK = 16


def _split(a):
    hi = a.astype(BF16)
    return hi, (a - hi.astype(F32)).astype(BF16)


def _split3(g):
    g1 = g.astype(BF16)
    r1 = g - g1.astype(F32)
    g2 = r1.astype(BF16)
    return g1, g2, (r1 - g2.astype(F32)).astype(BF16)


def _cumsum_rows(tri_b, g):
    g1, g2, g3 = _split3(g)
    return _dot(tri_b, g1) + (_dot(tri_b, g2) + _dot(tri_b, g3))


def _cumsum_cols(g, tri_t_b):
    g1, g2, g3 = _split3(g)
    return _dot(g1, tri_t_b) + (_dot(g2, tri_t_b) + _dot(g3, tri_t_b))


def _pair_lhs(a):
    hi, lo = _split(a)
    return jnp.concatenate([hi, lo, hi], axis=1)


def _pair_rhs(b, upper):
    bd = jnp.concatenate([jnp.where(upper, 0.0, b), jnp.where(upper, b, 0.0)], axis=0)
    hi, lo = _split(bd)
    return jnp.concatenate([hi, hi, lo], axis=0)


def _pair_solve_matrices(ms, eye, blk, upper):
    lhs = lambda vals: [_pair_lhs(v) for v in vals]
    rhs = lambda vals: [_pair_rhs(v, upper) for v in vals]
    mul = lambda ls, rs: [_dot(a, b) for a, b in zip(ls, rs)]
    C = eye.shape[0]
    ns = [jnp.where(blk, m, 0.0) for m in ms]
    xs = [eye - n for n in ns]
    ps = mul(lhs(ns), rhs(ns))
    steps = DN_BLOCK.bit_length() - 2
    for s in range(steps):
        pr = rhs(ps)
        if s + 1 < steps:
            both = mul([jnp.concatenate([a, b], axis=0) for a, b in zip(lhs(xs), lhs(ps))], pr)
            xs = [x + d[:C] for x, d in zip(xs, both)]
            ps = [d[C:] for d in both]
        else:
            xs = [x + d for x, d in zip(xs, mul(lhs(xs), pr))]
    xl = lhs(xs)
    qs = mul(xl, rhs([jnp.where(blk, 0.0, m) for m in ms]))
    ql = lhs(qs)
    q2 = mul(ql, rhs(qs))
    q3 = mul(ql, rhs(q2))
    rs = [eye - q + b - c for q, b, c in zip(qs, q2, q3)]
    return [t - eye for t in mul(lhs(rs), rhs(xs))]


def _gdn_chunk_pair(q_ref, k_ref, v_ref, z_ref, sm_ref, smt_ref, nw_ref, o_ref, state_ref):
    C = DN_CHUNK
    R = 2 * C
    assert C // DN_BLOCK == 4 and R == LANES and DN_HEAD_K == LANES and DN_HEAD_V == LANES
    heads = range(DN_HEADS)
    subs = range(2)

    ri = lax.broadcasted_iota(jnp.int32, (C, R), 0)
    cj = lax.broadcasted_iota(jnp.int32, (C, R), 1)
    upper = cj >= C
    cj = jnp.bitwise_and(cj, C - 1)
    incl = ri >= cj
    strict = ri > cj
    eye = (ri == cj).astype(F32)
    shift = DN_BLOCK.bit_length() - 1
    blk = (ri >> shift) == (cj >> shift)
    rr = lax.broadcasted_iota(jnp.int32, (R, R), 0)
    cc = lax.broadcasted_iota(jnp.int32, (R, R), 1)
    same = (rr >> (C.bit_length() - 1)) == (cc >> (C.bit_length() - 1))
    tri_b = jnp.logical_and(same, rr >= cc).astype(BF16)
    tri_t_b = jnp.logical_and(same, rr <= cc).astype(BF16)
    g_lane = jnp.logical_and(cc >= SM_G, cc < SM_G + DN_HEADS)
    g_row = jnp.logical_and(rr >= SM_G, rr < SM_G + DN_HEADS)

    rsl = lambda s: slice(s * C, (s + 1) * C)
    hsl = lambda h: slice(h * DN_HEAD_K, (h + 1) * DN_HEAD_K)
    gsl = lambda h: slice(SM_G + h, SM_G + h + 1)
    sm = sm_ref[...]
    gcum = _cumsum_rows(tri_b, jnp.where(g_lane, sm, 0.0))
    gcum_t = _cumsum_cols(jnp.where(g_row, smt_ref[...], 0.0), tri_t_b)
    eg = jnp.exp(gcum)
    glast = [gcum[(s + 1) * C - 1:(s + 1) * C, :] for s in subs]
    ekd = [jnp.exp(glast[s] - gcum[rsl(s), :]) for s in subs]
    eglast = [jnp.exp(g) for g in glast]
    nw = nw_ref[...]
    zero_b = jnp.zeros((C, LANES), BF16)

    kh = [[k_ref[rsl(s), hsl(h)] for s in subs] for h in heads]
    beta = [[sm[rsl(s), SM_BETA + h:SM_BETA + h + 1] for s in subs] for h in heads]
    kb = [[kh[h][s] * beta[h][s] for s in subs] for h in heads]
    k_nt = [jnp.concatenate([jnp.concatenate([kh[h][0].astype(BF16), zero_b], axis=1),
                             jnp.concatenate([zero_b, kh[h][1].astype(BF16)], axis=1)], axis=0)
            for h in heads]
    kq = [lax.dot_general(
        jnp.concatenate([jnp.concatenate(kb[h], axis=1),
                         jnp.concatenate([q_ref[rsl(s), hsl(h)] for s in subs], axis=1)], axis=0).astype(BF16),
        k_nt[h], NT_DIMS, preferred_element_type=F32) for h in heads]
    kk = [a[:C] for a in kq]
    qk = [a[C:] for a in kq]
    dec = []
    for h in heads:
        gc = jnp.where(upper, gcum[rsl(1), gsl(h)], gcum[rsl(0), gsl(h)])
        dec.append(jnp.exp(jnp.where(incl, gc - gcum_t[gsl(h), :], MASKED)))
    aqk = [(qk[h] * dec[h]).astype(BF16) for h in heads]
    tm1 = _pair_solve_matrices([jnp.where(strict, kk[h] * dec[h], 0.0) for h in heads], eye, blk, upper)
    tl = [jnp.concatenate(_split(t), axis=1) for t in tm1]
    rhs = [[jnp.concatenate([v_ref[rsl(s), hsl(h)] * beta[h][s], kb[h][s] * eg[rsl(s), gsl(h)]], axis=1)
            for s in subs] for h in heads]
    zero_r = jnp.zeros((C, DN_HEAD_V + DN_HEAD_K), BF16)
    rb = [[r.astype(BF16) for r in rhs[h]] for h in heads]
    pads = [[jnp.concatenate([rb[h][0], zero_r, rb[h][0], zero_r], axis=0),
             jnp.concatenate([zero_r, rb[h][1], zero_r, rb[h][1]], axis=0)] for h in heads]
    sol = [[rhs[h][s] + _dot(tl[h], pads[h][s]) for s in subs] for h in heads]
    qd = [[(q_ref[rsl(s), hsl(h)] * eg[rsl(s), gsl(h)]).astype(BF16) for s in subs] for h in heads]
    kd = [[(kh[h][s] * ekd[s][:, gsl(h)]).astype(BF16) for s in subs] for h in heads]

    st = [state_ref[h] for h in heads]
    for s in subs:
        st_b = [a.astype(BF16) for a in st]
        wq = [_dot(jnp.concatenate([sol[h][s][:, DN_HEAD_V:].astype(BF16), qd[h][s]], axis=0), st_b[h])
              for h in heads]
        qs = [a[C:] for a in wq]
        v_b = [(sol[h][s][:, :DN_HEAD_V] - wq[h][:C]).astype(BF16) for h in heads]
        zero_v = jnp.zeros((C, DN_HEAD_V), BF16)
        v_pad = [jnp.concatenate([v, zero_v] if s == 0 else [zero_v, v], axis=0) for v in v_b]
        o = [qs[h] + _dot(aqk[h], v_pad[h]) for h in heads]
        upd = [lax.dot_general(kd[h][s], v_b[h], TN_DIMS, preferred_element_type=F32) for h in heads]
        st = [st[h] * eglast[s][:, gsl(h)] + upd[h] for h in heads]
        for h in heads:
            ms_o = jnp.mean(o[h] * o[h], axis=-1, keepdims=True)
            o_ref[rsl(s), hsl(h)] = (o[h] * lax.rsqrt(ms_o + EPS) * nw * z_ref[rsl(s), hsl(h)]).astype(BF16)
    for h in heads:
        state_ref[h] = st[h]


def _mixers_kernel(x_ref, b_ref, c_ref, zs_ref, q_ref, k_ref, v_ref, zd_ref, sm_ref, smt_ref,
                   edt_ref, ela_ref, dskip_ref, ns_ref, nd_ref, ys_ref, yd_ref,
                   s_state_ref, yg_ref, d_state_ref, *, n_heads):
    @pl.when(pl.program_id(0) == 0)
    def _():
        s_state_ref[...] = jnp.zeros_like(s_state_ref)
        d_state_ref[...] = jnp.zeros_like(d_state_ref)

    _ssd_chunk(x_ref, b_ref, c_ref, zs_ref, sm_ref, smt_ref, edt_ref, ela_ref, dskip_ref, ns_ref,
                ys_ref, s_state_ref, yg_ref, n_heads=n_heads)
    _gdn_chunk_pair(q_ref, k_ref, v_ref, zd_ref, sm_ref, smt_ref, nd_ref, yd_ref, d_state_ref)


def _mixers(big, small, small_t, e_dt, e_la, dskip, ns, nd, layer, n_heads):
    S = big.shape[0]
    L = SSM_CHUNK
    inner = n_heads * SSM_HEAD_DIM
    gn = SSM_GROUPS * SSM_STATE
    W = DN_HEADS * DN_HEAD_K
    assert L == 2 * DN_CHUNK and L == LANES and W == PJ_TN
    const = lambda c: (0, 0)
    per_layer = lambda w: pl.BlockSpec((None, 1, w), lambda c: (layer, 0, 0))
    return pl.pallas_call(
        functools.partial(_mixers_kernel, n_heads=n_heads),
        out_shape=(jax.ShapeDtypeStruct((S, inner), BF16),
                   jax.ShapeDtypeStruct((S, DN_HEADS * DN_HEAD_V), BF16)),
        grid=(S // L,),
        in_specs=[
            pl.BlockSpec((L, inner), lambda c: (c, 0)),
            pl.BlockSpec((L, gn), lambda c: (c, inner // gn)),
            pl.BlockSpec((L, gn), lambda c: (c, inner // gn + 1)),
            pl.BlockSpec((L, inner), lambda c: (c, PJ_ZS)),
            pl.BlockSpec((L, W), lambda c: (c, PJ_Q)),
            pl.BlockSpec((L, W), lambda c: (c, PJ_K)),
            pl.BlockSpec((L, W), lambda c: (c, PJ_V)),
            pl.BlockSpec((L, W), lambda c: (c, PJ_ZD)),
            pl.BlockSpec((L, LANES), lambda c: (c, 0)),
            pl.BlockSpec((LANES, L), lambda c: (0, c)),
            pl.BlockSpec((2 * LANES, inner), const),
            pl.BlockSpec((2 * LANES, inner), const),
            per_layer(inner), per_layer(inner), per_layer(DN_HEAD_V),
        ],
        out_specs=(pl.BlockSpec((L, inner), lambda c: (c, 0)), pl.BlockSpec((L, W), lambda c: (c, 0))),
        scratch_shapes=[pltpu.VMEM((SSM_STATE, inner), F32), pltpu.VMEM((L, inner), F32),
                        pltpu.VMEM((DN_HEADS, DN_HEAD_K, DN_HEAD_V), F32)],
        compiler_params=_params("arbitrary"), name="mixers",
    )(big, big, big, big, big, big, big, big, small, small_t, e_dt, e_la, dskip, ns, nd)


def _mixout_kernel(h_ref, ys_ref, yd_ref, gs_ref, gd_ref, ws_ref, wd_ref, wo_ref, o_ref):
    merged = (_sigmoid(gs_ref[...]) * _dot(ys_ref[...], ws_ref[...])
              + _sigmoid(gd_ref[...]) * _dot(yd_ref[...], wd_ref[...]))
    o_ref[...] = h_ref[...] + _dot(merged.astype(BF16), wo_ref[...])


def _mixout(h, ys, yd, big, ws_b, wd_b, wo_b, layer, tm=512):
    S, D = h.shape
    assert D == PJ_TN
    row = lambda w: pl.BlockSpec((tm, w), lambda i: (i, 0))
    wspec = lambda a: pl.BlockSpec((None,) + a.shape[1:], lambda i: (layer, 0, 0))
    return pl.pallas_call(
        _mixout_kernel,
        out_shape=jax.ShapeDtypeStruct((S, D), F32),
        grid=(S // tm,),
        in_specs=[row(D), row(ys.shape[1]), row(yd.shape[1]),
                  pl.BlockSpec((tm, D), lambda i: (i, PJ_GS)),
                  pl.BlockSpec((tm, D), lambda i: (i, PJ_GD)),
                  wspec(ws_b), wspec(wd_b), wspec(wo_b)],
        out_specs=row(D),
        compiler_params=_params("parallel"), name="mixout",
    )(h, ys, yd, big, big, ws_b, wd_b, wo_b)


def kernel(x, ffn1_norm, ffn1_w_in, ffn1_w_out, mix_norm, w_in, ssm_conv_w, ssm_conv_b, ssm_dt_bias, ssm_a_log, ssm_d, ssm_norm, ssm_w_branch, dn_conv_w, dn_dt_bias, dn_a_log, dn_norm, dn_w_branch, w_out, ffn2_norm, ffn2_w_in, ffn2_w_out, final_norm):
    B, S, D = x.shape
    assert B == 1
    depth = w_in.shape[0]
    H = ssm_dt_bias.shape[1]
    inner = H * SSM_HEAD_DIM
    gn = SSM_GROUPS * SSM_STATE
    dn_k = DN_HEADS * DN_HEAD_K
    dn_v = DN_HEADS * DN_HEAD_V
    assert ssm_norm.shape[1] == inner and ssm_conv_w.shape[2] == inner + 2 * gn
    assert dn_conv_w.shape[2] == 2 * dn_k + dn_v and D == PJ_TN and dn_k == PJ_TN and dn_v == PJ_TN
    assert inner == 2 * PJ_TN and 2 * gn == PJ_TN and SM_LA + H <= LANES

    sizes = (inner, inner + 2 * gn, H, 2 * dn_k + dn_v, dn_v, DN_HEADS, DN_HEADS, D, D)
    offs = [0]
    for sz in sizes:
        offs.append(offs[-1] + sz)
    assert offs[-1] == w_in.shape[2]
    sl = lambda i: slice(offs[i], offs[i + 1])
    z_s, xbc, dt, qkv, z_d, b_d, a_d, g_s, g_d = (w_in[:, :, sl(i)] for i in range(9))

    w_big = jnp.concatenate([p.astype(BF16) for p in (xbc, qkv, z_s, z_d, g_s, g_d)], axis=-1)
    pad = jnp.zeros((depth, D, LANES - (SM_LA + H)), F32)
    w_sm = jnp.concatenate([dt, b_d, a_d, dt, pad], axis=-1).astype(BF16)
    w_sm_t = jnp.swapaxes(w_sm, 1, 2)
    zeros_h = jnp.zeros((depth, DN_HEADS), F32)
    tail = jnp.zeros((depth, LANES - (SM_LA + H)), F32)
    bias = jnp.concatenate([ssm_dt_bias, zeros_h, dn_dt_bias, ssm_dt_bias, tail], axis=-1)
    a_log = jnp.concatenate([jnp.zeros_like(ssm_a_log), zeros_h, dn_a_log, ssm_a_log, tail], axis=-1)
    kind = jnp.concatenate([jnp.zeros((H,), F32), jnp.ones((DN_HEADS,), F32),
                            jnp.full((DN_HEADS + H,), 2.0, F32), jnp.ones((LANES - (SM_LA + H),), F32)])
    kind = jnp.broadcast_to(kind, (depth, LANES))
    prow = jnp.stack([bias, a_log, kind] + [jnp.zeros_like(bias)] * (SUBLANES - 3), axis=1)
    pcol = jnp.swapaxes(prow, 1, 2)
    conv_w = jnp.concatenate([ssm_conv_w, dn_conv_w], axis=-1)
    conv_b = jnp.concatenate([ssm_conv_b, jnp.zeros((depth, 2 * dn_k + dn_v), F32)], axis=-1)[:, None, :]

    head_of_col = jnp.arange(inner) // SSM_HEAD_DIM
    lane_id = jnp.arange(2 * LANES) % LANES
    e_dt = (lane_id[:, None] == SM_DT + head_of_col[None, :]).astype(BF16)
    e_la = (lane_id[:, None] == SM_LA + head_of_col[None, :]).astype(BF16)
    dskip = jnp.repeat(ssm_d, SSM_HEAD_DIM, axis=-1)[:, None, :]

    v3 = lambda a: a[:, None, :]
    f1_in, f1_out = ffn1_w_in.astype(BF16), ffn1_w_out.astype(BF16)
    f2_in, f2_out = ffn2_w_in.astype(BF16), ffn2_w_out.astype(BF16)
    ws_b, wd_b, wo_b = ssm_w_branch.astype(BF16), dn_w_branch.astype(BF16), w_out.astype(BF16)
    n1, nm, n2, nf = v3(ffn1_norm), v3(mix_norm), v3(ffn2_norm), final_norm[None, None, :]
    ns, nd = v3(ssm_norm), v3(dn_norm)

    h = x.reshape(S, D)
    for l in range(depth):
        h, u, small, small_t = _ffn(h, n1, f1_in, f1_out, nm, l, "emit_u", (w_sm, w_sm_t, prow, pcol))
        big = _proj(u, w_big, conv_w, conv_b, l)
        ys, yd = _mixers(big, small, small_t, e_dt, e_la, dskip, ns, nd, l, H)
        h = _mixout(h, ys, yd, big, ws_b, wd_b, wo_b, l)
        last = l == depth - 1
        h = _ffn(h, n2, f2_in, f2_out, nf if last else n2, l, "final" if last else "plain")
    return h.reshape(B, S, D)
```

```python
import functools

import jax
import jax.numpy as jnp
from jax import lax
from jax.experimental import pallas as pl
from jax.experimental.pallas import tpu as pltpu

F32 = jnp.float32
BF16 = jnp.bfloat16

CONV_K = 4
SSM_HEAD_DIM = 64
SSM_GROUPS = 4
SSM_STATE = 128
SSM_CHUNK = 128
DN_HEADS = 8
DN_HEAD_K = 128
DN_HEAD_V = 128
DN_CHUNK = 64
EPS = 1e-6
MASKED = -1e30

LANES = 128
SUBLANES = 8
VMEM_LIMIT = 56 * 1024 * 1024

SM_DT = 0
SM_BETA = 32
SM_G = 40
SM_LA = 48

NT_DIMS = (((1,), (1,)), ((), ()))
TN_DIMS = (((0,), (0,)), ((), ()))


def _sigmoid(x):
    return 0.5 * jnp.tanh(0.5 * x) + 0.5


def _silu(x):
    h = 0.5 * x
    return h * jnp.tanh(h) + h


def _softplus(x):
    return jnp.maximum(x, 0.0) + jnp.log1p(jnp.exp(-jnp.abs(x)))


def _rms_norm(x, w):
    return x * lax.rsqrt(jnp.mean(x * x, axis=-1, keepdims=True) + EPS) * w


def _dot(a, b, **kw):
    return jnp.dot(a, b, preferred_element_type=F32, **kw)


def _params(*sem):
    return pltpu.CompilerParams(dimension_semantics=sem, vmem_limit_bytes=VMEM_LIMIT)


def _small_act_t(yt, bias, a_log, n_rows):
    sp = lambda lo, hi: _softplus(yt[lo:hi] + bias[lo:hi])
    return jnp.concatenate([
        sp(SM_DT, SM_BETA),
        _sigmoid(yt[SM_BETA:SM_G]),
        -jnp.exp(a_log[SM_G:n_rows]) * sp(SM_G, n_rows),
        jnp.zeros((yt.shape[0] - n_rows, yt.shape[1]), F32)], axis=0)


def _ffn_kernel(h_ref, nw_ref, win_ref, wout_ref, nw2_ref, *rest, d_ff, tf, mode, sm_rows=0):
    if mode == "emit_u":
        wsmt_ref, pcol_ref, o_ref, u_ref, sm_ref, smt_ref, act_ref = rest
    else:
        o_ref, act_ref = rest
    h = h_ref[...]
    xn = _rms_norm(h, nw_ref[...]).astype(BF16)
    for c in range(d_ff // tf):
        g = _dot(xn, win_ref[:, c * tf:(c + 1) * tf])
        u = _dot(xn, win_ref[:, d_ff + c * tf:d_ff + (c + 1) * tf])
        act_ref[:, c * tf:(c + 1) * tf] = (_silu(g) * u).astype(BF16)
    hn = h + 0.5 * _dot(act_ref[...], wout_ref[...])
    if mode == "emit_u":
        o_ref[...] = hn
        u = _rms_norm(hn, nw2_ref[...]).astype(BF16)
        u_ref[...] = u
        yt = lax.dot_general(wsmt_ref[...], u, NT_DIMS, preferred_element_type=F32)
        act_t = _small_act_t(yt, pcol_ref[:, 0:1], pcol_ref[:, 1:2], sm_rows)
        smt_ref[...] = act_t
        sm_ref[...] = act_t.T
    elif mode == "final":
        o_ref[...] = _rms_norm(hn, nw2_ref[...])
    else:
        o_ref[...] = hn


def _ffn(h, nw, win_b, wout_b, nw2, layer, mode, small=(), sm_rows=0, tm=1024, tf=256):
    S, D = h.shape
    d_ff = wout_b.shape[1]
    row = pl.BlockSpec((tm, D), lambda i: (i, 0))
    per_layer = lambda *shape: pl.BlockSpec((None,) + shape, lambda i: (layer, 0, 0))
    once = pl.Buffered(1)
    in_specs = [
        row,
        per_layer(1, D),
        pl.BlockSpec((None, D, 2 * d_ff), lambda i: (layer, 0, 0), pipeline_mode=once),
        pl.BlockSpec((None, d_ff, D), lambda i: (layer, 0, 0), pipeline_mode=once),
        pl.BlockSpec((None, 1, D), lambda i: (nw2.shape[0] - 1 if mode == "final" else layer, 0, 0)),
    ]
    if mode == "emit_u":
        in_specs += [per_layer(LANES, D), per_layer(LANES, SUBLANES)]
        out_shape = (jax.ShapeDtypeStruct((S, D), F32), jax.ShapeDtypeStruct((S, D), BF16),
                     jax.ShapeDtypeStruct((S, LANES), F32), jax.ShapeDtypeStruct((LANES, S), F32))
        out_specs = (row, row, pl.BlockSpec((tm, LANES), lambda i: (i, 0)),
                     pl.BlockSpec((LANES, tm), lambda i: (0, i)))
    else:
        assert not small
        out_shape = jax.ShapeDtypeStruct((S, D), F32)
        out_specs = row
    return pl.pallas_call(
        functools.partial(_ffn_kernel, d_ff=d_ff, tf=tf, mode=mode, sm_rows=sm_rows),
        out_shape=out_shape, grid=(S // tm,), in_specs=in_specs, out_specs=out_specs,
        scratch_shapes=[pltpu.VMEM((tm, d_ff), BF16)],
        compiler_params=_params("parallel"), name="ffn_" + mode,
    )(h, nw, win_b, wout_b, nw2, *small)


PJ_TN = 1024
PJ_SUB = 256
PJ_ROWS = 256
PJ_CONV_BLOCKS = 6
PJ_Q, PJ_K, PJ_V = 3, 4, 5
PJ_ZS, PJ_ZD, PJ_GS, PJ_GD = 3, 8, 9, 10
PJ_SILU_END = 9
PJ_BLOCKS = 11


def _proj_kernel(u_ref, w_ref, cw_ref, cb_ref, o_ref, ext_ref, *, tm):
    j = pl.program_id(0)
    i = pl.program_id(1)
    is_conv = j < PJ_CONV_BLOCKS
    is_qk = jnp.logical_or(j == PJ_Q, j == PJ_K)
    n_row_units = tm // PJ_ROWS
    units = [(r, c) for c in range(PJ_TN // PJ_SUB) for r in range(n_row_units)]
    rows = lambda r: slice(r * PJ_ROWS, (r + 1) * PJ_ROWS)
    cols = lambda c: slice(c * PJ_SUB, (c + 1) * PJ_SUB)
    lane_cols = lambda c: range(c * PJ_SUB // LANES, (c + 1) * PJ_SUB // LANES)

    @pl.when(jnp.logical_and(is_conv, i == 0))
    def _():
        ext_ref[0, :, 0:SUBLANES, :] = jnp.zeros((PJ_TN // LANES, SUBLANES, LANES), F32)

    def pipelined(epilogue):
        matmul = lambda r, c: _dot(u_ref[rows(r), :], w_ref[:, cols(c)])
        y_next = matmul(*units[0])
        for n, (r, c) in enumerate(units):
            y = y_next
            if n + 1 < len(units):
                y_next = matmul(*units[n + 1])
            for g, lc in enumerate(lane_cols(c)):
                epilogue(r, lc, y[:, g * LANES:(g + 1) * LANES])

    def conv_silu(r, lc, y):
        ls = slice(lc * LANES, (lc + 1) * LANES)
        ext_ref[r, lc, SUBLANES:, :] = y
        h = 0.5 * cb_ref[:, ls] + (0.5 * cw_ref[CONV_K - 1:CONV_K, ls]) * y
        for k in range(CONV_K - 1):
            off = SUBLANES - (CONV_K - 1) + k
            h = h + (0.5 * cw_ref[k:k + 1, ls]) * ext_ref[r, lc, off:off + PJ_ROWS, :]
        ext_ref[(r + 1) % n_row_units, lc, 0:SUBLANES, :] = y[PJ_ROWS - SUBLANES:, :]
        return h * jnp.tanh(h) + h

    @pl.when(jnp.logical_and(is_conv, jnp.logical_not(is_qk)))
    def _():
        def epilogue(r, lc, y):
            o_ref[rows(r), lc * LANES:(lc + 1) * LANES] = conv_silu(r, lc, y)
        pipelined(epilogue)

    @pl.when(is_qk)
    def _():
        scale = jnp.where(j == PJ_Q, DN_HEAD_K ** -0.5, 1.0).astype(F32)

        def epilogue(r, lc, y):
            a = conv_silu(r, lc, y)
            ss = jnp.sum(a * a, axis=-1, keepdims=True)
            o_ref[rows(r), lc * LANES:(lc + 1) * LANES] = a * (lax.rsqrt(ss + EPS) * scale)
        pipelined(epilogue)

    @pl.when(jnp.logical_and(jnp.logical_not(is_conv), j < PJ_SILU_END))
    def _():
        def epilogue(r, lc, y):
            o_ref[rows(r), lc * LANES:(lc + 1) * LANES] = _silu(y)
        pipelined(epilogue)

    @pl.when(j >= PJ_SILU_END)
    def _():
        def epilogue(r, lc, y):
            o_ref[rows(r), lc * LANES:(lc + 1) * LANES] = y
        pipelined(epilogue)


def _proj(u, w_big, conv_w, conv_b, layer, tm=2048):
    S, D = u.shape
    last_conv = PJ_CONV_BLOCKS - 1
    return pl.pallas_call(
        functools.partial(_proj_kernel, tm=tm),
        out_shape=jax.ShapeDtypeStruct((S, PJ_BLOCKS * PJ_TN), F32),
        grid=(PJ_BLOCKS, S // tm),
        in_specs=[
            pl.BlockSpec((tm, D), lambda j, i: (i, 0)),
            pl.BlockSpec((None, D, PJ_TN), lambda j, i: (layer, 0, j)),
            pl.BlockSpec((None, CONV_K, PJ_TN), lambda j, i: (layer, 0, jnp.minimum(j, last_conv))),
            pl.BlockSpec((None, 1, PJ_TN), lambda j, i: (layer, 0, jnp.minimum(j, last_conv))),
        ],
        out_specs=pl.BlockSpec((tm, PJ_TN), lambda j, i: (i, j)),
        scratch_shapes=[pltpu.VMEM((tm // PJ_ROWS, PJ_TN // LANES, PJ_ROWS + SUBLANES, LANES), F32)],
        compiler_params=_params("arbitrary", "arbitrary"), name="proj",
    )(u, w_big, conv_w, conv_b)


def _split2(a):
    hi = a.astype(BF16)
    lo = (a - hi.astype(F32)).astype(BF16)
    return jnp.concatenate([hi, lo], axis=1)


def _ssd_chunk(x_ref, b_ref, c_ref, z_ref, sm_ref, smt_ref, ela_ref, dskip_ref,
               nw_ref, o_ref, state_ref, yg_ref, *, n_heads):
    L = SSM_CHUNK
    P = SSM_HEAD_DIM
    N = SSM_STATE
    hg = n_heads // SSM_GROUPS
    gw = hg * P

    row = lax.broadcasted_iota(jnp.int32, (L, L), 0)
    col = lax.broadcasted_iota(jnp.int32, (L, L), 1)
    causal = row >= col
    la_lane = jnp.logical_and(col >= SM_LA, col < SM_LA + n_heads)
    la_row = jnp.logical_and(row >= SM_LA, row < SM_LA + n_heads)

    sm = sm_ref[...]
    smt = smt_ref[...]
    acum = _cumsum_rows(causal.astype(BF16), jnp.where(la_lane, sm, 0.0))
    acum_t = _cumsum_cols(jnp.where(la_row, smt, 0.0), (row <= col).astype(BF16))
    eacum = jnp.exp(acum)
    dtdte = pltpu.roll(sm, SM_LA - SM_DT, 1) * jnp.exp(acum[L - 1:L, :] - acum)
    e2 = _dot(_split2(jnp.concatenate([dtdte, eacum], axis=0)), ela_ref[...])
    e_dtdte = e2[:L]
    e_ea = e2[L:]

    x = x_ref[...]
    xsc_b = (x * e_dtdte).astype(BF16)
    low_half = col < P
    bm = b_ref[...].astype(BF16)
    cm = c_ref[...].astype(BF16)
    dskip = dskip_ref[...]
    nw = nw_ref[...]
    for g in range(SSM_GROUPS):
        gs = slice(g * gw, (g + 1) * gw)
        bg = bm[:, g * N:(g + 1) * N]
        cg = cm[:, g * N:(g + 1) * N]
        cb = lax.dot_general(cg, bg, NT_DIMS, preferred_element_type=F32)
        st = state_ref[:, gs]
        yoff = _dot(cg, st.astype(BF16)) * e_ea[:, gs]
        state_ref[:, gs] = st * e_ea[L - 1:L, gs] + lax.dot_general(
            bg, xsc_b[:, gs], TN_DIMS, preferred_element_type=F32)
        for pp in range(hg // 2):
            h0 = g * hg + 2 * pp
            ws = []
            for hh in (h0, h0 + 1):
                seg = acum[:, SM_LA + hh:SM_LA + hh + 1] - acum_t[SM_LA + hh:SM_LA + hh + 1, :]
                ws.append((cb * jnp.exp(jnp.where(causal, seg, MASKED))
                           * smt[SM_DT + hh:SM_DT + hh + 1, :]).astype(BF16))
            lhs = jnp.concatenate(ws, axis=1)
            cs = slice(h0 * P, (h0 + 2) * P)
            xp = x[:, cs]
            rhs = jnp.concatenate([jnp.where(low_half, xp, 0.0), jnp.where(low_half, 0.0, xp)],
                                  axis=0).astype(BF16)
            y = _dot(lhs, rhs) + yoff[:, pp * 2 * P:(pp + 1) * 2 * P] + x[:, cs] * dskip[:, cs]
            yg_ref[:, cs] = y * z_ref[:, cs]
        yg = yg_ref[:, gs]
        ms = jnp.sum(yg * yg, axis=-1, keepdims=True) * (1.0 / gw)
        o_ref[:, gs] = (yg * lax.rsqrt(ms + EPS) * nw[:, gs]).astype(BF16)


DN_BLOCK = 16


def _split(a):
    hi = a.astype(BF16)
    return hi, (a - hi.astype(F32)).astype(BF16)


def _split3(g):
    g1 = g.astype(BF16)
    r1 = g - g1.astype(F32)
    g2 = r1.astype(BF16)
    return g1, g2, (r1 - g2.astype(F32)).astype(BF16)


def _cumsum_rows(tri_b, g):
    g1, g2, g3 = _split3(g)
    return _dot(tri_b, g1) + (_dot(tri_b, g2) + _dot(tri_b, g3))


def _cumsum_cols(g, tri_t_b):
    g1, g2, g3 = _split3(g)
    return _dot(g1, tri_t_b) + (_dot(g2, tri_t_b) + _dot(g3, tri_t_b))


def _pair_lhs(a):
    hi, lo = _split(a)
    return jnp.concatenate([hi, lo, hi], axis=1)


def _pair_rhs(b, upper):
    bd = jnp.concatenate([jnp.where(upper, 0.0, b), jnp.where(upper, b, 0.0)], axis=0)
    hi, lo = _split(bd)
    return jnp.concatenate([hi, hi, lo], axis=0)


def _pair_solve_matrices(ms, eye, blk, upper):
    lhs = lambda vals: [_pair_lhs(v) for v in vals]
    rhs = lambda vals: [_pair_rhs(v, upper) for v in vals]
    mul = lambda ls, rs: [_dot(a, b) for a, b in zip(ls, rs)]
    C = eye.shape[0]
    ns = [jnp.where(blk, m, 0.0) for m in ms]
    xs = [eye - n for n in ns]
    ps = mul(lhs(ns), rhs(ns))
    steps = DN_BLOCK.bit_length() - 2
    for s in range(steps):
        pr = rhs(ps)
        if s + 1 < steps:
            both = mul([jnp.concatenate([a, b], axis=0) for a, b in zip(lhs(xs), lhs(ps))], pr)
            xs = [x + d[:C] for x, d in zip(xs, both)]
            ps = [d[C:] for d in both]
        else:
            xs = [x + d for x, d in zip(xs, mul(lhs(xs), pr))]
    xl = lhs(xs)
    qs = mul(xl, rhs([jnp.where(blk, 0.0, m) for m in ms]))
    ql = lhs(qs)
    q2 = mul(ql, rhs(qs))
    q3 = mul(ql, rhs(q2))
    rs = [eye - q + b - c for q, b, c in zip(qs, q2, q3)]
    return [t - eye for t in mul(lhs(rs), rhs(xs))]


def _gdn_chunk_pair(q_ref, k_ref, v_ref, z_ref, sm_ref, smt_ref, nw_ref, o_ref, state_ref):
    C = DN_CHUNK
    R = 2 * C
    assert C // DN_BLOCK == 4 and R == LANES and DN_HEAD_K == LANES and DN_HEAD_V == LANES
    heads = range(DN_HEADS)
    subs = range(2)

    ri = lax.broadcasted_iota(jnp.int32, (C, R), 0)
    cj = lax.broadcasted_iota(jnp.int32, (C, R), 1)
    upper = cj >= C
    cj = jnp.bitwise_and(cj, C - 1)
    incl = ri >= cj
    strict = ri > cj
    eye = (ri == cj).astype(F32)
    shift = DN_BLOCK.bit_length() - 1
    blk = (ri >> shift) == (cj >> shift)
    rr = lax.broadcasted_iota(jnp.int32, (R, R), 0)
    cc = lax.broadcasted_iota(jnp.int32, (R, R), 1)
    same = (rr >> (C.bit_length() - 1)) == (cc >> (C.bit_length() - 1))
    tri_b = jnp.logical_and(same, rr >= cc).astype(BF16)
    tri_t_b = jnp.logical_and(same, rr <= cc).astype(BF16)
    g_lane = jnp.logical_and(cc >= SM_G, cc < SM_G + DN_HEADS)
    g_row = jnp.logical_and(rr >= SM_G, rr < SM_G + DN_HEADS)

    rsl = lambda s: slice(s * C, (s + 1) * C)
    hsl = lambda h: slice(h * DN_HEAD_K, (h + 1) * DN_HEAD_K)
    gsl = lambda h: slice(SM_G + h, SM_G + h + 1)
    sm = sm_ref[...]
    gcum = _cumsum_rows(tri_b, jnp.where(g_lane, sm, 0.0))
    gcum_t = _cumsum_cols(jnp.where(g_row, smt_ref[...], 0.0), tri_t_b)
    eg = jnp.exp(gcum)
    glast = [gcum[(s + 1) * C - 1:(s + 1) * C, :] for s in subs]
    ekd = [jnp.exp(glast[s] - gcum[rsl(s), :]) for s in subs]
    eglast = [jnp.exp(g) for g in glast]
    nw = nw_ref[...]
    zero_b = jnp.zeros((C, LANES), BF16)

    kh = [[k_ref[rsl(s), hsl(h)] for s in subs] for h in heads]
    beta = [[sm[rsl(s), SM_BETA + h:SM_BETA + h + 1] for s in subs] for h in heads]
    kb = [[kh[h][s] * beta[h][s] for s in subs] for h in heads]
    k_nt = [jnp.concatenate([jnp.concatenate([kh[h][0].astype(BF16), zero_b], axis=1),
                             jnp.concatenate([zero_b, kh[h][1].astype(BF16)], axis=1)], axis=0)
            for h in heads]
    kq = [lax.dot_general(
        jnp.concatenate([jnp.concatenate(kb[h], axis=1),
                         jnp.concatenate([q_ref[rsl(s), hsl(h)] for s in subs], axis=1)], axis=0).astype(BF16),
        k_nt[h], NT_DIMS, preferred_element_type=F32) for h in heads]
    kk = [a[:C] for a in kq]
    qk = [a[C:] for a in kq]
    dec = []
    for h in heads:
        gc = jnp.where(upper, gcum[rsl(1), gsl(h)], gcum[rsl(0), gsl(h)])
        dec.append(jnp.exp(jnp.where(incl, gc - gcum_t[gsl(h), :], MASKED)))
    aqk = [(qk[h] * dec[h]).astype(BF16) for h in heads]
    tm1 = _pair_solve_matrices([jnp.where(strict, kk[h] * dec[h], 0.0) for h in heads], eye, blk, upper)
    tl = [jnp.concatenate(_split(t), axis=1) for t in tm1]
    rhs = [[jnp.concatenate([v_ref[rsl(s), hsl(h)] * beta[h][s], kb[h][s] * eg[rsl(s), gsl(h)]], axis=1)
            for s in subs] for h in heads]
    zero_r = jnp.zeros((C, DN_HEAD_V + DN_HEAD_K), BF16)
    rb = [[r.astype(BF16) for r in rhs[h]] for h in heads]
    pads = [[jnp.concatenate([rb[h][0], zero_r, rb[h][0], zero_r], axis=0),
             jnp.concatenate([zero_r, rb[h][1], zero_r, rb[h][1]], axis=0)] for h in heads]
    sol = [[rhs[h][s] + _dot(tl[h], pads[h][s]) for s in subs] for h in heads]
    qd = [[(q_ref[rsl(s), hsl(h)] * eg[rsl(s), gsl(h)]).astype(BF16) for s in subs] for h in heads]
    kd = [[(kh[h][s] * ekd[s][:, gsl(h)]).astype(BF16) for s in subs] for h in heads]

    st = [state_ref[h] for h in heads]
    for s in subs:
        st_b = [a.astype(BF16) for a in st]
        wq = [_dot(jnp.concatenate([sol[h][s][:, DN_HEAD_V:].astype(BF16), qd[h][s]], axis=0), st_b[h])
              for h in heads]
        qs = [a[C:] for a in wq]
        v_b = [(sol[h][s][:, :DN_HEAD_V] - wq[h][:C]).astype(BF16) for h in heads]
        zero_v = jnp.zeros((C, DN_HEAD_V), BF16)
        v_pad = [jnp.concatenate([v, zero_v] if s == 0 else [zero_v, v], axis=0) for v in v_b]
        o = [qs[h] + _dot(aqk[h], v_pad[h]) for h in heads]
        upd = [lax.dot_general(kd[h][s], v_b[h], TN_DIMS, preferred_element_type=F32) for h in heads]
        st = [st[h] * eglast[s][:, gsl(h)] + upd[h] for h in heads]
        for h in heads:
            ms_o = jnp.mean(o[h] * o[h], axis=-1, keepdims=True)
            o_ref[rsl(s), hsl(h)] = (o[h] * lax.rsqrt(ms_o + EPS) * nw * z_ref[rsl(s), hsl(h)]).astype(BF16)
    for h in heads:
        state_ref[h] = st[h]


def _mixers_kernel(x_ref, b_ref, c_ref, zs_ref, q_ref, k_ref, v_ref, zd_ref, sm_ref, smt_ref,
                   ela_ref, dskip_ref, ns_ref, nd_ref, ys_ref, yd_ref,
                   s_state_ref, yg_ref, d_state_ref, *, n_heads):
    @pl.when(pl.program_id(0) == 0)
    def _():
        s_state_ref[...] = jnp.zeros_like(s_state_ref)
        d_state_ref[...] = jnp.zeros_like(d_state_ref)

    _ssd_chunk(x_ref, b_ref, c_ref, zs_ref, sm_ref, smt_ref, ela_ref, dskip_ref, ns_ref,
               ys_ref, s_state_ref, yg_ref, n_heads=n_heads)
    _gdn_chunk_pair(q_ref, k_ref, v_ref, zd_ref, sm_ref, smt_ref, nd_ref, yd_ref, d_state_ref)


def _mixers(big, small, small_t, e_la, dskip, ns, nd, layer, n_heads):
    S = big.shape[0]
    L = SSM_CHUNK
    inner = n_heads * SSM_HEAD_DIM
    gn = SSM_GROUPS * SSM_STATE
    W = DN_HEADS * DN_HEAD_K
    assert L == 2 * DN_CHUNK and L == LANES and W == PJ_TN
    const = lambda c: (0, 0)
    per_layer = lambda w: pl.BlockSpec((None, 1, w), lambda c: (layer, 0, 0))
    return pl.pallas_call(
        functools.partial(_mixers_kernel, n_heads=n_heads),
        out_shape=(jax.ShapeDtypeStruct((S, inner), BF16),
                   jax.ShapeDtypeStruct((S, DN_HEADS * DN_HEAD_V), BF16)),
        grid=(S // L,),
        in_specs=[
            pl.BlockSpec((L, inner), lambda c: (c, 0)),
            pl.BlockSpec((L, gn), lambda c: (c, inner // gn)),
            pl.BlockSpec((L, gn), lambda c: (c, inner // gn + 1)),
            pl.BlockSpec((L, inner), lambda c: (c, PJ_ZS)),
            pl.BlockSpec((L, W), lambda c: (c, PJ_Q)),
            pl.BlockSpec((L, W), lambda c: (c, PJ_K)),
            pl.BlockSpec((L, W), lambda c: (c, PJ_V)),
            pl.BlockSpec((L, W), lambda c: (c, PJ_ZD)),
            pl.BlockSpec((L, LANES), lambda c: (c, 0)),
            pl.BlockSpec((LANES, L), lambda c: (0, c)),
            pl.BlockSpec((2 * LANES, inner), const),
            per_layer(inner), per_layer(inner), per_layer(DN_HEAD_V),
        ],
        out_specs=(pl.BlockSpec((L, inner), lambda c: (c, 0)), pl.BlockSpec((L, W), lambda c: (c, 0))),
        scratch_shapes=[pltpu.VMEM((SSM_STATE, inner), F32), pltpu.VMEM((L, inner), F32),
                        pltpu.VMEM((DN_HEADS, DN_HEAD_K, DN_HEAD_V), F32)],
        compiler_params=_params("arbitrary"), name="mixers",
    )(big, big, big, big, big, big, big, big, small, small_t, e_la, dskip, ns, nd)


def _mixout_kernel(h_ref, ys_ref, yd_ref, gs_ref, gd_ref, ws_ref, wd_ref, wo_ref, o_ref):
    merged = (_sigmoid(gs_ref[...]) * _dot(ys_ref[...], ws_ref[...])
              + _sigmoid(gd_ref[...]) * _dot(yd_ref[...], wd_ref[...]))
    o_ref[...] = h_ref[...] + _dot(merged.astype(BF16), wo_ref[...])


def _mixout(h, ys, yd, big, ws_b, wd_b, wo_b, layer, tm=512):
    S, D = h.shape
    assert D == PJ_TN
    row = lambda w: pl.BlockSpec((tm, w), lambda i: (i, 0))
    wspec = lambda a: pl.BlockSpec((None,) + a.shape[1:], lambda i: (layer, 0, 0))
    return pl.pallas_call(
        _mixout_kernel,
        out_shape=jax.ShapeDtypeStruct((S, D), F32),
        grid=(S // tm,),
        in_specs=[row(D), row(ys.shape[1]), row(yd.shape[1]),
                  pl.BlockSpec((tm, D), lambda i: (i, PJ_GS)),
                  pl.BlockSpec((tm, D), lambda i: (i, PJ_GD)),
                  wspec(ws_b), wspec(wd_b), wspec(wo_b)],
        out_specs=row(D),
        compiler_params=_params("parallel"), name="mixout",
    )(h, ys, yd, big, big, ws_b, wd_b, wo_b)


def kernel(x, ffn1_norm, ffn1_w_in, ffn1_w_out, mix_norm, w_in, ssm_conv_w, ssm_conv_b, ssm_dt_bias, ssm_a_log, ssm_d, ssm_norm, ssm_w_branch, dn_conv_w, dn_dt_bias, dn_a_log, dn_norm, dn_w_branch, w_out, ffn2_norm, ffn2_w_in, ffn2_w_out, final_norm):
    B, S, D = x.shape
    assert B == 1
    depth = w_in.shape[0]
    H = ssm_dt_bias.shape[1]
    inner = H * SSM_HEAD_DIM
    gn = SSM_GROUPS * SSM_STATE
    dn_k = DN_HEADS * DN_HEAD_K
    dn_v = DN_HEADS * DN_HEAD_V
    assert ssm_norm.shape[1] == inner and ssm_conv_w.shape[2] == inner + 2 * gn
    assert dn_conv_w.shape[2] == 2 * dn_k + dn_v and D == PJ_TN and dn_k == PJ_TN and dn_v == PJ_TN
    assert inner == 2 * PJ_TN and 2 * gn == PJ_TN and SM_LA + H <= LANES

    sizes = (inner, inner + 2 * gn, H, 2 * dn_k + dn_v, dn_v, DN_HEADS, DN_HEADS, D, D)
    offs = [0]
    for sz in sizes:
        offs.append(offs[-1] + sz)
    assert offs[-1] == w_in.shape[2]
    sl = lambda i: slice(offs[i], offs[i + 1])
    z_s, xbc, dt, qkv, z_d, b_d, a_d, g_s, g_d = (w_in[:, :, sl(i)] for i in range(9))

    w_big = jnp.concatenate([p.astype(BF16) for p in (xbc, qkv, z_s, z_d, g_s, g_d)], axis=-1)
    pad = jnp.zeros((depth, D, LANES - (SM_LA + H)), F32)
    w_sm = jnp.concatenate([dt, b_d, a_d, dt, pad], axis=-1).astype(BF16)
    w_sm_t = jnp.swapaxes(w_sm, 1, 2)
    zeros_h = jnp.zeros((depth, DN_HEADS), F32)
    tail = jnp.zeros((depth, LANES - (SM_LA + H)), F32)
    bias = jnp.concatenate([ssm_dt_bias, zeros_h, dn_dt_bias, ssm_dt_bias, tail], axis=-1)
    a_log = jnp.concatenate([jnp.zeros_like(ssm_a_log), zeros_h, dn_a_log, ssm_a_log, tail], axis=-1)
    pcol = jnp.stack([bias, a_log] + [jnp.zeros_like(bias)] * (SUBLANES - 2), axis=2)
    conv_w = jnp.concatenate([ssm_conv_w, dn_conv_w], axis=-1)
    conv_b = jnp.concatenate([ssm_conv_b, jnp.zeros((depth, 2 * dn_k + dn_v), F32)], axis=-1)[:, None, :]

    head_of_col = jnp.arange(inner) // SSM_HEAD_DIM
    lane_id = jnp.arange(2 * LANES) % LANES
    e_la = (lane_id[:, None] == SM_LA + head_of_col[None, :]).astype(BF16)
    dskip = jnp.repeat(ssm_d, SSM_HEAD_DIM, axis=-1)[:, None, :]

    v3 = lambda a: a[:, None, :]
    f1_in, f1_out = ffn1_w_in.astype(BF16), ffn1_w_out.astype(BF16)
    f2_in, f2_out = ffn2_w_in.astype(BF16), ffn2_w_out.astype(BF16)
    ws_b, wd_b, wo_b = ssm_w_branch.astype(BF16), dn_w_branch.astype(BF16), w_out.astype(BF16)
    n1, nm, n2, nf = v3(ffn1_norm), v3(mix_norm), v3(ffn2_norm), final_norm[None, None, :]
    ns, nd = v3(ssm_norm), v3(dn_norm)

    h = x.reshape(S, D)
    for l in range(depth):
        h, u, small, small_t = _ffn(h, n1, f1_in, f1_out, nm, l, "emit_u", (w_sm_t, pcol), SM_LA + H)
        big = _proj(u, w_big, conv_w, conv_b, l)
        ys, yd = _mixers(big, small, small_t, e_la, dskip, ns, nd, l, H)
        h = _mixout(h, ys, yd, big, ws_b, wd_b, wo_b, l)
        last = l == depth - 1
        h = _ffn(h, n2, f2_in, f2_out, nf if last else n2, l, "final" if last else "plain")
    return h.reshape(B, S, D)
```

```python
import functools

import jax
import jax.numpy as jnp
from jax import lax
from jax.experimental import pallas as pl
from jax.experimental.pallas import tpu as pltpu

F32 = jnp.float32
BF16 = jnp.bfloat16

CONV_K = 4
SSM_HEAD_DIM = 64
SSM_GROUPS = 4
SSM_STATE = 128
SSM_CHUNK = 128
DN_HEADS = 8
DN_HEAD_K = 128
DN_HEAD_V = 128
DN_CHUNK = 64
EPS = 1e-6
MASKED = -1e30

LANES = 128
SUBLANES = 8
VMEM_LIMIT = 56 * 1024 * 1024

SM_DT = 0
SM_BETA = 32
SM_G = 40
SM_LA = 48

NT_DIMS = (((1,), (1,)), ((), ()))
TN_DIMS = (((0,), (0,)), ((), ()))


def _sigmoid(x):
    return 0.5 * jnp.tanh(0.5 * x) + 0.5


def _silu(x):
    h = 0.5 * x
    return h * jnp.tanh(h) + h


def _softplus(x):
    return jnp.maximum(x, 0.0) + jnp.log1p(jnp.exp(-jnp.abs(x)))


def _rms_norm(x, w):
    return x * lax.rsqrt(jnp.mean(x * x, axis=-1, keepdims=True) + EPS) * w


def _dot(a, b, **kw):
    return jnp.dot(a, b, preferred_element_type=F32, **kw)


def _params(*sem):
    return pltpu.CompilerParams(dimension_semantics=sem, vmem_limit_bytes=VMEM_LIMIT)


def _small_act_t(yt, bias, a_log, n_rows):
    sp = lambda lo, hi: _softplus(yt[lo:hi] + bias[lo:hi])
    return jnp.concatenate([
        sp(SM_DT, SM_BETA),
        _sigmoid(yt[SM_BETA:SM_G]),
        -jnp.exp(a_log[SM_G:n_rows]) * sp(SM_G, n_rows),
        jnp.zeros((yt.shape[0] - n_rows, yt.shape[1]), F32)], axis=0)


def _ffn_kernel(h_ref, nw_ref, win_ref, wout_ref, nw2_ref, *rest, d_ff, tf, mode, sm_rows=0):
    if mode == "emit_u":
        wsmt_ref, pcol_ref, o_ref, u_ref, sm_ref, smt_ref, act_ref = rest
    else:
        o_ref, act_ref = rest
    h = h_ref[...]
    xn = _rms_norm(h, nw_ref[...]).astype(BF16)
    for c in range(d_ff // tf):
        g = _dot(xn, win_ref[:, c * tf:(c + 1) * tf])
        u = _dot(xn, win_ref[:, d_ff + c * tf:d_ff + (c + 1) * tf])
        act_ref[:, c * tf:(c + 1) * tf] = (_silu(g) * u).astype(BF16)
    hn = h + 0.5 * _dot(act_ref[...], wout_ref[...])
    if mode == "emit_u":
        o_ref[...] = hn
        u = _rms_norm(hn, nw2_ref[...]).astype(BF16)
        u_ref[...] = u
        yt = lax.dot_general(wsmt_ref[...], u, NT_DIMS, preferred_element_type=F32)
        act_t = _small_act_t(yt, pcol_ref[:, 0:1], pcol_ref[:, 1:2], sm_rows)
        smt_ref[...] = act_t
        sm_ref[...] = act_t.T
    elif mode == "final":
        o_ref[...] = _rms_norm(hn, nw2_ref[...])
    else:
        o_ref[...] = hn


def _ffn(h, nw, win_b, wout_b, nw2, layer, mode, small=(), sm_rows=0, tm=1024, tf=256):
    S, D = h.shape
    d_ff = wout_b.shape[1]
    row = pl.BlockSpec((tm, D), lambda i: (i, 0))
    per_layer = lambda *shape: pl.BlockSpec((None,) + shape, lambda i: (layer, 0, 0))
    once = pl.Buffered(1)
    in_specs = [
        row,
        per_layer(1, D),
        pl.BlockSpec((None, D, 2 * d_ff), lambda i: (layer, 0, 0), pipeline_mode=once),
        pl.BlockSpec((None, d_ff, D), lambda i: (layer, 0, 0), pipeline_mode=once),
        pl.BlockSpec((None, 1, D), lambda i: (nw2.shape[0] - 1 if mode == "final" else layer, 0, 0)),
    ]
    if mode == "emit_u":
        in_specs += [per_layer(LANES, D), per_layer(LANES, SUBLANES)]
        out_shape = (jax.ShapeDtypeStruct((S, D), F32), jax.ShapeDtypeStruct((S, D), BF16),
                     jax.ShapeDtypeStruct((S, LANES), F32), jax.ShapeDtypeStruct((LANES, S), F32))
        out_specs = (row, row, pl.BlockSpec((tm, LANES), lambda i: (i, 0)),
                     pl.BlockSpec((LANES, tm), lambda i: (0, i)))
    else:
        assert not small
        out_shape = jax.ShapeDtypeStruct((S, D), F32)
        out_specs = row
    return pl.pallas_call(
        functools.partial(_ffn_kernel, d_ff=d_ff, tf=tf, mode=mode, sm_rows=sm_rows),
        out_shape=out_shape, grid=(S // tm,), in_specs=in_specs, out_specs=out_specs,
        scratch_shapes=[pltpu.VMEM((tm, d_ff), BF16)],
        compiler_params=_params("parallel"), name="ffn_" + mode,
    )(h, nw, win_b, wout_b, nw2, *small)


PJ_TN = 1024
PJ_SUB = 256
PJ_ROWS = 256
PJ_CONV_BLOCKS = 6
PJ_Q, PJ_K, PJ_V = 3, 4, 5
PJ_ZS, PJ_ZD, PJ_GS, PJ_GD = 3, 8, 9, 10
PJ_SILU_END = 9
PJ_BLOCKS = 11


def _proj_kernel(u_ref, w_ref, cw_ref, cb_ref, o_ref, ext_ref, tail_ref, *, tm):
    i = pl.program_id(0)
    j = pl.program_id(1)
    is_conv = j < PJ_CONV_BLOCKS
    jc = jnp.minimum(j, PJ_CONV_BLOCKS - 1)
    is_qk = jnp.logical_or(j == PJ_Q, j == PJ_K)
    n_row_units = tm // PJ_ROWS
    units = [(r, c) for c in range(PJ_TN // PJ_SUB) for r in range(n_row_units)]
    rows = lambda r: slice(r * PJ_ROWS, (r + 1) * PJ_ROWS)
    cols = lambda c: slice(c * PJ_SUB, (c + 1) * PJ_SUB)
    lane_cols = lambda c: range(c * PJ_SUB // LANES, (c + 1) * PJ_SUB // LANES)

    @pl.when(jnp.logical_and(is_conv, i == 0))
    def _():
        tail_ref[jc] = jnp.zeros((PJ_TN // LANES, SUBLANES, LANES), F32)

    def pipelined(epilogue, conv=False):
        if conv:
            ext_ref[0, :, 0:SUBLANES, :] = tail_ref[jc]
        matmul = lambda r, c: _dot(u_ref[rows(r), :], w_ref[:, cols(c)])
        y_next = matmul(*units[0])
        for n, (r, c) in enumerate(units):
            y = y_next
            if n + 1 < len(units):
                y_next = matmul(*units[n + 1])
            for g, lc in enumerate(lane_cols(c)):
                epilogue(r, lc, y[:, g * LANES:(g + 1) * LANES])
        if conv:
            tail_ref[jc] = ext_ref[0, :, 0:SUBLANES, :]

    def conv_silu(r, lc, y):
        ls = slice(lc * LANES, (lc + 1) * LANES)
        ext_ref[r, lc, SUBLANES:, :] = y
        h = 0.5 * cb_ref[:, ls] + (0.5 * cw_ref[CONV_K - 1:CONV_K, ls]) * y
        for k in range(CONV_K - 1):
            off = SUBLANES - (CONV_K - 1) + k
            h = h + (0.5 * cw_ref[k:k + 1, ls]) * ext_ref[r, lc, off:off + PJ_ROWS, :]
        ext_ref[(r + 1) % n_row_units, lc, 0:SUBLANES, :] = y[PJ_ROWS - SUBLANES:, :]
        return h * jnp.tanh(h) + h

    @pl.when(jnp.logical_and(is_conv, jnp.logical_not(is_qk)))
    def _():
        def epilogue(r, lc, y):
            o_ref[rows(r), lc * LANES:(lc + 1) * LANES] = conv_silu(r, lc, y)
        pipelined(epilogue, conv=True)

    @pl.when(is_qk)
    def _():
        scale = jnp.where(j == PJ_Q, DN_HEAD_K ** -0.5, 1.0).astype(F32)

        def epilogue(r, lc, y):
            a = conv_silu(r, lc, y)
            ss = jnp.sum(a * a, axis=-1, keepdims=True)
            o_ref[rows(r), lc * LANES:(lc + 1) * LANES] = a * (lax.rsqrt(ss + EPS) * scale)
        pipelined(epilogue, conv=True)

    @pl.when(jnp.logical_and(jnp.logical_not(is_conv), j < PJ_SILU_END))
    def _():
        def epilogue(r, lc, y):
            o_ref[rows(r), lc * LANES:(lc + 1) * LANES] = _silu(y)
        pipelined(epilogue)

    @pl.when(j >= PJ_SILU_END)
    def _():
        def epilogue(r, lc, y):
            o_ref[rows(r), lc * LANES:(lc + 1) * LANES] = y
        pipelined(epilogue)


def _proj(u, w_big, conv_w, conv_b, layer, tm=2048):
    S, D = u.shape
    last_conv = PJ_CONV_BLOCKS - 1
    return pl.pallas_call(
        functools.partial(_proj_kernel, tm=tm),
        out_shape=jax.ShapeDtypeStruct((S, PJ_BLOCKS * PJ_TN), F32),
        grid=(S // tm, PJ_BLOCKS),
        in_specs=[
            pl.BlockSpec((tm, D), lambda i, j: (i, 0)),
            pl.BlockSpec((None, D, PJ_TN), lambda i, j: (layer, 0, j)),
            pl.BlockSpec((None, CONV_K, PJ_TN), lambda i, j: (layer, 0, jnp.minimum(j, last_conv))),
            pl.BlockSpec((None, 1, PJ_TN), lambda i, j: (layer, 0, jnp.minimum(j, last_conv))),
        ],
        out_specs=pl.BlockSpec((tm, PJ_TN), lambda i, j: (i, j)),
        scratch_shapes=[pltpu.VMEM((tm // PJ_ROWS, PJ_TN // LANES, PJ_ROWS + SUBLANES, LANES), F32),
                        pltpu.VMEM((PJ_CONV_BLOCKS, PJ_TN // LANES, SUBLANES, LANES), F32)],
        compiler_params=_params("arbitrary", "arbitrary"), name="proj",
    )(u, w_big, conv_w, conv_b)


def _split2(a):
    hi = a.astype(BF16)
    lo = (a - hi.astype(F32)).astype(BF16)
    return jnp.concatenate([hi, lo], axis=1)


def _ssd_chunk(x_ref, b_ref, c_ref, z_ref, sm_ref, smt_ref, ela_ref, dskip_ref,
               nw_ref, o_ref, state_ref, yg_ref, *, n_heads):
    L = SSM_CHUNK
    P = SSM_HEAD_DIM
    N = SSM_STATE
    hg = n_heads // SSM_GROUPS
    gw = hg * P

    row = lax.broadcasted_iota(jnp.int32, (L, L), 0)
    col = lax.broadcasted_iota(jnp.int32, (L, L), 1)
    causal = row >= col
    la_lane = jnp.logical_and(col >= SM_LA, col < SM_LA + n_heads)
    la_row = jnp.logical_and(row >= SM_LA, row < SM_LA + n_heads)

    sm = sm_ref[...]
    smt = smt_ref[...]
    acum = _cumsum_rows(causal.astype(BF16), jnp.where(la_lane, sm, 0.0))
    acum_t = _cumsum_cols(jnp.where(la_row, smt, 0.0), (row <= col).astype(BF16))
    eacum = jnp.exp(acum)
    dtdte = pltpu.roll(sm, SM_LA - SM_DT, 1) * jnp.exp(acum[L - 1:L, :] - acum)
    e2 = _dot(_split2(jnp.concatenate([dtdte, eacum], axis=0)), ela_ref[...])
    e_dtdte = e2[:L]
    e_ea = e2[L:]

    x = x_ref[...]
    xsc_b = (x * e_dtdte).astype(BF16)
    low_half = col < P
    bm = b_ref[...].astype(BF16)
    cm = c_ref[...].astype(BF16)
    dskip = dskip_ref[...]
    nw = nw_ref[...]
    for g in range(SSM_GROUPS):
        gs = slice(g * gw, (g + 1) * gw)
        bg = bm[:, g * N:(g + 1) * N]
        cg = cm[:, g * N:(g + 1) * N]
        cb = lax.dot_general(cg, bg, NT_DIMS, preferred_element_type=F32)
        st = state_ref[:, gs]
        yoff = _dot(cg, st.astype(BF16)) * e_ea[:, gs]
        state_ref[:, gs] = st * e_ea[L - 1:L, gs] + lax.dot_general(
            bg, xsc_b[:, gs], TN_DIMS, preferred_element_type=F32)
        for pp in range(hg // 2):
            h0 = g * hg + 2 * pp
            ws = []
            for hh in (h0, h0 + 1):
                seg = acum[:, SM_LA + hh:SM_LA + hh + 1] - acum_t[SM_LA + hh:SM_LA + hh + 1, :]
                ws.append((cb * jnp.exp(jnp.where(causal, seg, MASKED))
                           * smt[SM_DT + hh:SM_DT + hh + 1, :]).astype(BF16))
            lhs = jnp.concatenate(ws, axis=1)
            cs = slice(h0 * P, (h0 + 2) * P)
            xp = x[:, cs]
            rhs = jnp.concatenate([jnp.where(low_half, xp, 0.0), jnp.where(low_half, 0.0, xp)],
                                  axis=0).astype(BF16)
            y = _dot(lhs, rhs) + yoff[:, pp * 2 * P:(pp + 1) * 2 * P] + x[:, cs] * dskip[:, cs]
            yg_ref[:, cs] = y * z_ref[:, cs]
        yg = yg_ref[:, gs]
        ms = jnp.sum(yg * yg, axis=-1, keepdims=True) * (1.0 / gw)
        o_ref[:, gs] = (yg * lax.rsqrt(ms + EPS) * nw[:, gs]).astype(BF16)


DN_BLOCK = 16


def _split(a):
    hi = a.astype(BF16)
    return hi, (a - hi.astype(F32)).astype(BF16)


def _split3(g):
    g1 = g.astype(BF16)
    r1 = g - g1.astype(F32)
    g2 = r1.astype(BF16)
    return g1, g2, (r1 - g2.astype(F32)).astype(BF16)


def _cumsum_rows(tri_b, g):
    g1, g2, g3 = _split3(g)
    return _dot(tri_b, g1) + (_dot(tri_b, g2) + _dot(tri_b, g3))


def _cumsum_cols(g, tri_t_b):
    g1, g2, g3 = _split3(g)
    return _dot(g1, tri_t_b) + (_dot(g2, tri_t_b) + _dot(g3, tri_t_b))


def _pair_lhs(a):
    hi, lo = _split(a)
    return jnp.concatenate([hi, lo, hi], axis=1)


def _pair_rhs(b, upper):
    bd = jnp.concatenate([jnp.where(upper, 0.0, b), jnp.where(upper, b, 0.0)], axis=0)
    hi, lo = _split(bd)
    return jnp.concatenate([hi, hi, lo], axis=0)


def _pair_solve_matrices(ms, eye, blk, upper):
    lhs = lambda vals: [_pair_lhs(v) for v in vals]
    rhs = lambda vals: [_pair_rhs(v, upper) for v in vals]
    mul = lambda ls, rs: [_dot(a, b) for a, b in zip(ls, rs)]
    C = eye.shape[0]
    ns = [jnp.where(blk, m, 0.0) for m in ms]
    xs = [eye - n for n in ns]
    ps = mul(lhs(ns), rhs(ns))
    steps = DN_BLOCK.bit_length() - 2
    for s in range(steps):
        pr = rhs(ps)
        if s + 1 < steps:
            both = mul([jnp.concatenate([a, b], axis=0) for a, b in zip(lhs(xs), lhs(ps))], pr)
            xs = [x + d[:C] for x, d in zip(xs, both)]
            ps = [d[C:] for d in both]
        else:
            xs = [x + d for x, d in zip(xs, mul(lhs(xs), pr))]
    xl = lhs(xs)
    qs = mul(xl, rhs([jnp.where(blk, 0.0, m) for m in ms]))
    ql = lhs(qs)
    q2 = mul(ql, rhs(qs))
    q3 = mul(ql, rhs(q2))
    rs = [eye - q + b - c for q, b, c in zip(qs, q2, q3)]
    return [t - eye for t in mul(lhs(rs), rhs(xs))]


def _gdn_chunk_pair(q_ref, k_ref, v_ref, z_ref, sm_ref, smt_ref, nw_ref, o_ref, state_ref):
    C = DN_CHUNK
    R = 2 * C
    assert C // DN_BLOCK == 4 and R == LANES and DN_HEAD_K == LANES and DN_HEAD_V == LANES
    heads = range(DN_HEADS)
    subs = range(2)

    ri = lax.broadcasted_iota(jnp.int32, (C, R), 0)
    cj = lax.broadcasted_iota(jnp.int32, (C, R), 1)
    upper = cj >= C
    cj = jnp.bitwise_and(cj, C - 1)
    incl = ri >= cj
    strict = ri > cj
    eye = (ri == cj).astype(F32)
    shift = DN_BLOCK.bit_length() - 1
    blk = (ri >> shift) == (cj >> shift)
    rr = lax.broadcasted_iota(jnp.int32, (R, R), 0)
    cc = lax.broadcasted_iota(jnp.int32, (R, R), 1)
    same = (rr >> (C.bit_length() - 1)) == (cc >> (C.bit_length() - 1))
    tri_b = jnp.logical_and(same, rr >= cc).astype(BF16)
    tri_t_b = jnp.logical_and(same, rr <= cc).astype(BF16)
    g_lane = jnp.logical_and(cc >= SM_G, cc < SM_G + DN_HEADS)
    g_row = jnp.logical_and(rr >= SM_G, rr < SM_G + DN_HEADS)

    rsl = lambda s: slice(s * C, (s + 1) * C)
    hsl = lambda h: slice(h * DN_HEAD_K, (h + 1) * DN_HEAD_K)
    gsl = lambda h: slice(SM_G + h, SM_G + h + 1)
    sm = sm_ref[...]
    gcum = _cumsum_rows(tri_b, jnp.where(g_lane, sm, 0.0))
    gcum_t = _cumsum_cols(jnp.where(g_row, smt_ref[...], 0.0), tri_t_b)
    eg = jnp.exp(gcum)
    glast = [gcum[(s + 1) * C - 1:(s + 1) * C, :] for s in subs]
    ekd = [jnp.exp(glast[s] - gcum[rsl(s), :]) for s in subs]
    eglast = [jnp.exp(g) for g in glast]
    nw = nw_ref[...]
    zero_b = jnp.zeros((C, LANES), BF16)

    kh = [[k_ref[rsl(s), hsl(h)] for s in subs] for h in heads]
    beta = [[sm[rsl(s), SM_BETA + h:SM_BETA + h + 1] for s in subs] for h in heads]
    kb = [[kh[h][s] * beta[h][s] for s in subs] for h in heads]
    k_nt = [jnp.concatenate([jnp.concatenate([kh[h][0].astype(BF16), zero_b], axis=1),
                             jnp.concatenate([zero_b, kh[h][1].astype(BF16)], axis=1)], axis=0)
            for h in heads]
    kq = [lax.dot_general(
        jnp.concatenate([jnp.concatenate(kb[h], axis=1),
                         jnp.concatenate([q_ref[rsl(s), hsl(h)] for s in subs], axis=1)], axis=0).astype(BF16),
        k_nt[h], NT_DIMS, preferred_element_type=F32) for h in heads]
    kk = [a[:C] for a in kq]
    qk = [a[C:] for a in kq]
    dec = []
    for h in heads:
        gc = jnp.where(upper, gcum[rsl(1), gsl(h)], gcum[rsl(0), gsl(h)])
        dec.append(jnp.exp(jnp.where(incl, gc - gcum_t[gsl(h), :], MASKED)))
    aqk = [(qk[h] * dec[h]).astype(BF16) for h in heads]
    tm1 = _pair_solve_matrices([jnp.where(strict, kk[h] * dec[h], 0.0) for h in heads], eye, blk, upper)
    tl = [jnp.concatenate(_split(t), axis=1) for t in tm1]
    rhs = [[jnp.concatenate([v_ref[rsl(s), hsl(h)] * beta[h][s], kb[h][s] * eg[rsl(s), gsl(h)]], axis=1)
            for s in subs] for h in heads]
    zero_r = jnp.zeros((C, DN_HEAD_V + DN_HEAD_K), BF16)
    rb = [[r.astype(BF16) for r in rhs[h]] for h in heads]
    pads = [[jnp.concatenate([rb[h][0], zero_r, rb[h][0], zero_r], axis=0),
             jnp.concatenate([zero_r, rb[h][1], zero_r, rb[h][1]], axis=0)] for h in heads]
    sol = [[rhs[h][s] + _dot(tl[h], pads[h][s]) for s in subs] for h in heads]
    qd = [[(q_ref[rsl(s), hsl(h)] * eg[rsl(s), gsl(h)]).astype(BF16) for s in subs] for h in heads]
    kd = [[(kh[h][s] * ekd[s][:, gsl(h)]).astype(BF16) for s in subs] for h in heads]

    st = [state_ref[h] for h in heads]
    for s in subs:
        st_b = [a.astype(BF16) for a in st]
        wq = [_dot(jnp.concatenate([sol[h][s][:, DN_HEAD_V:].astype(BF16), qd[h][s]], axis=0), st_b[h])
              for h in heads]
        qs = [a[C:] for a in wq]
        v_b = [(sol[h][s][:, :DN_HEAD_V] - wq[h][:C]).astype(BF16) for h in heads]
        zero_v = jnp.zeros((C, DN_HEAD_V), BF16)
        v_pad = [jnp.concatenate([v, zero_v] if s == 0 else [zero_v, v], axis=0) for v in v_b]
        o = [qs[h] + _dot(aqk[h], v_pad[h]) for h in heads]
        upd = [lax.dot_general(kd[h][s], v_b[h], TN_DIMS, preferred_element_type=F32) for h in heads]
        st = [st[h] * eglast[s][:, gsl(h)] + upd[h] for h in heads]
        for h in heads:
            ms_o = jnp.mean(o[h] * o[h], axis=-1, keepdims=True)
            o_ref[rsl(s), hsl(h)] = (o[h] * lax.rsqrt(ms_o + EPS) * nw * z_ref[rsl(s), hsl(h)]).astype(BF16)
    for h in heads:
        state_ref[h] = st[h]


def _mixers_kernel(x_ref, b_ref, c_ref, zs_ref, q_ref, k_ref, v_ref, zd_ref, sm_ref, smt_ref,
                   ela_ref, dskip_ref, ns_ref, nd_ref, ys_ref, yd_ref,
                   s_state_ref, yg_ref, d_state_ref, *, n_heads):
    @pl.when(pl.program_id(0) == 0)
    def _():
        s_state_ref[...] = jnp.zeros_like(s_state_ref)
        d_state_ref[...] = jnp.zeros_like(d_state_ref)

    _ssd_chunk(x_ref, b_ref, c_ref, zs_ref, sm_ref, smt_ref, ela_ref, dskip_ref, ns_ref,
               ys_ref, s_state_ref, yg_ref, n_heads=n_heads)
    _gdn_chunk_pair(q_ref, k_ref, v_ref, zd_ref, sm_ref, smt_ref, nd_ref, yd_ref, d_state_ref)


def _mixers(big, small, small_t, e_la, dskip, ns, nd, layer, n_heads):
    S = big.shape[0]
    L = SSM_CHUNK
    inner = n_heads * SSM_HEAD_DIM
    gn = SSM_GROUPS * SSM_STATE
    W = DN_HEADS * DN_HEAD_K
    assert L == 2 * DN_CHUNK and L == LANES and W == PJ_TN
    const = lambda c: (0, 0)
    per_layer = lambda w: pl.BlockSpec((None, 1, w), lambda c: (layer, 0, 0))
    stream = pl.BlockSpec
    return pl.pallas_call(
        functools.partial(_mixers_kernel, n_heads=n_heads),
        out_shape=(jax.ShapeDtypeStruct((S, inner), BF16),
                   jax.ShapeDtypeStruct((S, DN_HEADS * DN_HEAD_V), BF16)),
        grid=(S // L,),
        in_specs=[
            stream((L, inner), lambda c: (c, 0)),
            stream((L, gn), lambda c: (c, inner // gn)),
            stream((L, gn), lambda c: (c, inner // gn + 1)),
            stream((L, inner), lambda c: (c, PJ_ZS)),
            stream((L, W), lambda c: (c, PJ_Q)),
            stream((L, W), lambda c: (c, PJ_K)),
            stream((L, W), lambda c: (c, PJ_V)),
            stream((L, W), lambda c: (c, PJ_ZD)),
            pl.BlockSpec((L, LANES), lambda c: (c, 0)),
            pl.BlockSpec((LANES, L), lambda c: (0, c)),
            pl.BlockSpec((2 * LANES, inner), const),
            per_layer(inner), per_layer(inner), per_layer(DN_HEAD_V),
        ],
        out_specs=(pl.BlockSpec((L, inner), lambda c: (c, 0)), pl.BlockSpec((L, W), lambda c: (c, 0))),
        scratch_shapes=[pltpu.VMEM((SSM_STATE, inner), F32), pltpu.VMEM((L, inner), F32),
                        pltpu.VMEM((DN_HEADS, DN_HEAD_K, DN_HEAD_V), F32)],
        compiler_params=_params("arbitrary"), name="mixers",
    )(big, big, big, big, big, big, big, big, small, small_t, e_la, dskip, ns, nd)


def _mixout_kernel(h_ref, ys_ref, yd_ref, gs_ref, gd_ref, ws_ref, wd_ref, wo_ref, o_ref):
    merged = (_sigmoid(gs_ref[...]) * _dot(ys_ref[...], ws_ref[...])
              + _sigmoid(gd_ref[...]) * _dot(yd_ref[...], wd_ref[...]))
    o_ref[...] = h_ref[...] + _dot(merged.astype(BF16), wo_ref[...])


def _mixout(h, ys, yd, big, ws_b, wd_b, wo_b, layer, tm=512):
    S, D = h.shape
    assert D == PJ_TN
    row = lambda w: pl.BlockSpec((tm, w), lambda i: (i, 0))
    wspec = lambda a: pl.BlockSpec((None,) + a.shape[1:], lambda i: (layer, 0, 0))
    return pl.pallas_call(
        _mixout_kernel,
        out_shape=jax.ShapeDtypeStruct((S, D), F32),
        grid=(S // tm,),
        in_specs=[row(D), row(ys.shape[1]), row(yd.shape[1]),
                  pl.BlockSpec((tm, D), lambda i: (i, PJ_GS)),
                  pl.BlockSpec((tm, D), lambda i: (i, PJ_GD)),
                  wspec(ws_b), wspec(wd_b), wspec(wo_b)],
        out_specs=row(D),
        compiler_params=_params("parallel"), name="mixout",
    )(h, ys, yd, big, big, ws_b, wd_b, wo_b)


def kernel(x, ffn1_norm, ffn1_w_in, ffn1_w_out, mix_norm, w_in, ssm_conv_w, ssm_conv_b, ssm_dt_bias, ssm_a_log, ssm_d, ssm_norm, ssm_w_branch, dn_conv_w, dn_dt_bias, dn_a_log, dn_norm, dn_w_branch, w_out, ffn2_norm, ffn2_w_in, ffn2_w_out, final_norm):
    B, S, D = x.shape
    assert B == 1
    depth = w_in.shape[0]
    H = ssm_dt_bias.shape[1]
    inner = H * SSM_HEAD_DIM
    gn = SSM_GROUPS * SSM_STATE
    dn_k = DN_HEADS * DN_HEAD_K
    dn_v = DN_HEADS * DN_HEAD_V
    assert ssm_norm.shape[1] == inner and ssm_conv_w.shape[2] == inner + 2 * gn
    assert dn_conv_w.shape[2] == 2 * dn_k + dn_v and D == PJ_TN and dn_k == PJ_TN and dn_v == PJ_TN
    assert inner == 2 * PJ_TN and 2 * gn == PJ_TN and SM_LA + H <= LANES

    sizes = (inner, inner + 2 * gn, H, 2 * dn_k + dn_v, dn_v, DN_HEADS, DN_HEADS, D, D)
    offs = [0]
    for sz in sizes:
        offs.append(offs[-1] + sz)
    assert offs[-1] == w_in.shape[2]
    sl = lambda i: slice(offs[i], offs[i + 1])
    z_s, xbc, dt, qkv, z_d, b_d, a_d, g_s, g_d = (w_in[:, :, sl(i)] for i in range(9))

    w_big = jnp.concatenate([p.astype(BF16) for p in (xbc, qkv, z_s, z_d, g_s, g_d)], axis=-1)
    pad = jnp.zeros((depth, D, LANES - (SM_LA + H)), F32)
    w_sm = jnp.concatenate([dt, b_d, a_d, dt, pad], axis=-1).astype(BF16)
    w_sm_t = jnp.swapaxes(w_sm, 1, 2)
    zeros_h = jnp.zeros((depth, DN_HEADS), F32)
    tail = jnp.zeros((depth, LANES - (SM_LA + H)), F32)
    bias = jnp.concatenate([ssm_dt_bias, zeros_h, dn_dt_bias, ssm_dt_bias, tail], axis=-1)
    a_log = jnp.concatenate([jnp.zeros_like(ssm_a_log), zeros_h, dn_a_log, ssm_a_log, tail], axis=-1)
    pcol = jnp.stack([bias, a_log] + [jnp.zeros_like(bias)] * (SUBLANES - 2), axis=2)
    conv_w = jnp.concatenate([ssm_conv_w, dn_conv_w], axis=-1)
    conv_b = jnp.concatenate([ssm_conv_b, jnp.zeros((depth, 2 * dn_k + dn_v), F32)], axis=-1)[:, None, :]

    head_of_col = jnp.arange(inner) // SSM_HEAD_DIM
    lane_id = jnp.arange(2 * LANES) % LANES
    e_la = (lane_id[:, None] == SM_LA + head_of_col[None, :]).astype(BF16)
    dskip = jnp.repeat(ssm_d, SSM_HEAD_DIM, axis=-1)[:, None, :]

    v3 = lambda a: a[:, None, :]
    f1_in, f1_out = ffn1_w_in.astype(BF16), ffn1_w_out.astype(BF16)
    f2_in, f2_out = ffn2_w_in.astype(BF16), ffn2_w_out.astype(BF16)
    ws_b, wd_b, wo_b = ssm_w_branch.astype(BF16), dn_w_branch.astype(BF16), w_out.astype(BF16)
    n1, nm, n2, nf = v3(ffn1_norm), v3(mix_norm), v3(ffn2_norm), final_norm[None, None, :]
    ns, nd = v3(ssm_norm), v3(dn_norm)

    h = x.reshape(S, D)
    for l in range(depth):
        h, u, small, small_t = _ffn(h, n1, f1_in, f1_out, nm, l, "emit_u", (w_sm_t, pcol), SM_LA + H)
        big = _proj(u, w_big, conv_w, conv_b, l)
        ys, yd = _mixers(big, small, small_t, e_la, dskip, ns, nd, l, H)
        h = _mixout(h, ys, yd, big, ws_b, wd_b, wo_b, l)
        last = l == depth - 1
        h = _ffn(h, n2, f2_in, f2_out, nf if last else n2, l, "final" if last else "plain")
    return h.reshape(B, S, D)
```
